```python
import math
import jax, jax.numpy as jnp
from jax import lax
import numpy as np

D_MODEL = 2048
BATCH = 1
SEQ = 16384
DEPTH = 1

CHUNK = 128
SGU_GROUPS = 8
SGU_GROUP_DIM = 128
SGU_WIDTH = SGU_GROUPS * SGU_GROUP_DIM
SB_HEADS = 8
SB_HEAD_DIM = 128
SB_WIDTH = SB_HEADS * SB_HEAD_DIM
Q_BLOCK = 128
N_BRANCH = 2
D_FF = ((8 * D_MODEL // 3 + 255) // 256) * 256
EPS = 1e-6

OFF_U = 0
OFF_V = OFF_U + SGU_WIDTH
OFF_Q = OFF_V + SGU_WIDTH
OFF_K = OFF_Q + SB_WIDTH
OFF_VB = OFF_K + SB_WIDTH
OFF_G = OFF_VB + SB_WIDTH
IN_COLS = OFF_G + N_BRANCH * D_MODEL

kernel_name = "hybrid_gmlp_stickbreaking_gated_block"


def rmsnorm(x, g):
    xf = x.astype(jnp.float32)
    y = xf * lax.rsqrt(jnp.mean(xf * xf, axis=-1, keepdims=True) + EPS)
    return (y * g.astype(jnp.float32)).astype(x.dtype)


def layernorm(x, g, b):
    xf = x.astype(jnp.float32)
    mu = jnp.mean(xf, axis=-1, keepdims=True)
    var = jnp.mean(jnp.square(xf - mu), axis=-1, keepdims=True)
    y = (xf - mu) * lax.rsqrt(var + EPS)
    return (y * g.astype(jnp.float32) + b.astype(jnp.float32)).astype(x.dtype)


def spatial_gating(u, v, ln_g, ln_b, w_s, b_s):
    B, S, _ = u.shape
    v = layernorm(v, ln_g, ln_b)
    vb = v.reshape(B, S // CHUNK, CHUNK, SGU_GROUPS, SGU_GROUP_DIM)
    tril = jnp.tril(jnp.ones((CHUNK, CHUNK), dtype=w_s.dtype))
    w_masked = w_s * tril[None]
    mixed = jnp.einsum('gts,bcsgd->bctgd', w_masked, vb) + b_s.T[None, None, :, :, None]
    return u * mixed.reshape(B, S, SGU_WIDTH)


def stick_breaking_attention(q, k, v):
    B, S, H, dh = q.shape
    n_blocks = S // Q_BLOCK
    scale = 1.0 / math.sqrt(dh)
    kf = jnp.transpose(k, (0, 2, 1, 3)).astype(jnp.float32)
    vh = jnp.transpose(v, (0, 2, 1, 3))
    qb = q.reshape(B, n_blocks, Q_BLOCK, H, dh).transpose(1, 0, 3, 2, 4)
    key_pos = jnp.arange(S, dtype=jnp.int32)

    def one_block(args):
        q_blk, blk = args
        z = jnp.einsum('bhqd,bhkd->bhqk', q_blk.astype(jnp.float32), kf) * scale
        q_pos = blk * Q_BLOCK + jnp.arange(Q_BLOCK, dtype=jnp.int32)
        causal = key_pos[None, :] < q_pos[:, None]
        log_beta = jnp.where(causal, jax.nn.log_sigmoid(z), -jnp.inf)
        log_1mb = jnp.where(causal, jax.nn.log_sigmoid(-z), 0.0)
        shifted = jnp.pad(log_1mb[..., 1:], ((0, 0), (0, 0), (0, 0), (0, 1)))
        excl = lax.cumsum(shifted, axis=3, reverse=True)
        w = jnp.exp(log_beta + excl).astype(vh.dtype)
        return jnp.einsum('bhqk,bhkd->bhqd', w, vh)

    out = lax.map(one_block, (qb, jnp.arange(n_blocks, dtype=jnp.int32)))
    return out.transpose(1, 0, 3, 2, 4).reshape(B, S, H * dh)


def setup_inputs(seed: int = 0) -> dict:
    key = jax.random.key(seed)
    ks = jax.random.split(key, 16)
    L, D = DEPTH, D_MODEL
    nrm = lambda k, shape, fan_in: jax.random.normal(k, shape, jnp.float32) * (fan_in ** -0.5)
    gain = lambda k, shape: 1.0 + 0.01 * jax.random.normal(k, shape, jnp.float32)
    return {
        "x": jax.random.normal(ks[0], (BATCH, SEQ, D), jnp.float32),
        "norm_mix": gain(ks[1], (L, D)),
        "w_in": nrm(ks[2], (L, D, IN_COLS), D),
        "b_gate": 0.01 * jax.random.normal(ks[3], (L, N_BRANCH * D), jnp.float32),
        "ln_v_g": gain(ks[4], (L, SGU_WIDTH)),
        "ln_v_b": 0.01 * jax.random.normal(ks[5], (L, SGU_WIDTH), jnp.float32),
        "w_s": 0.5 * nrm(ks[6], (L, SGU_GROUPS, CHUNK, CHUNK), CHUNK),
        "b_s": gain(ks[7], (L, SGU_GROUPS, CHUNK)),
        "w_proj_a": nrm(ks[8], (L, SGU_WIDTH, D), SGU_WIDTH),
        "w_proj_b": nrm(ks[9], (L, SB_WIDTH, D), SB_WIDTH),
        "w_out": nrm(ks[10], (L, D, D), D),
        "norm_ffn": gain(ks[11], (L, D)),
        "w_gate_up": nrm(ks[12], (L, D, 2 * D_FF), D),
        "w_down": nrm(ks[13], (L, D_FF, D), D_FF),
        "norm_final": gain(ks[14], (D,)),
    }


def reference(x, norm_mix, w_in, b_gate, ln_v_g, ln_v_b, w_s, b_s, w_proj_a, w_proj_b,
              w_out, norm_ffn, w_gate_up, w_down, norm_final):
    B, S, D = x.shape
    for l in range(DEPTH):
        xn = rmsnorm(x, norm_mix[l])
        proj = xn @ w_in[l]
        u = jax.nn.gelu(proj[..., OFF_U:OFF_V], approximate=False)
        v_a = jax.nn.gelu(proj[..., OFF_V:OFF_Q], approximate=False)
        q = proj[..., OFF_Q:OFF_K].reshape(B, S, SB_HEADS, SB_HEAD_DIM)
        k = proj[..., OFF_K:OFF_VB].reshape(B, S, SB_HEADS, SB_HEAD_DIM)
        v_b = proj[..., OFF_VB:OFF_G].reshape(B, S, SB_HEADS, SB_HEAD_DIM)
        gates = jax.nn.sigmoid(proj[..., OFF_G:] + b_gate[l])
        g_a, g_b = gates[..., :D], gates[..., D:]

        y_a = spatial_gating(u, v_a, ln_v_g[l], ln_v_b[l], w_s[l], b_s[l]) @ w_proj_a[l]
        y_b = stick_breaking_attention(q, k, v_b) @ w_proj_b[l]
        x = x + (g_a * y_a + g_b * y_b) @ w_out[l]

        hn = rmsnorm(x, norm_ffn[l])
        gu = hn @ w_gate_up[l]
        h = jax.nn.silu(gu[..., :D_FF]) * gu[..., D_FF:]
        x = x + h @ w_down[l]
    return rmsnorm(x, norm_final)
```

```python
import functools
import math

import jax
import jax.numpy as jnp
from jax import lax
from jax.experimental import pallas as pl
from jax.experimental.pallas import tpu as pltpu

EPS = 1e-6
CHUNK = 128
SGU_GROUPS = 8
SGU_GROUP_DIM = 128
SB_HEADS = 8
SB_HEAD_DIM = 128
ATT_BLOCK = 256
LOG_F32_ZERO = -104.0
VMEM_LIMIT_BYTES = 56 * 1024 * 1024

F32 = jnp.float32
BF16 = jnp.bfloat16


def _params(*sem):
    return pltpu.CompilerParams(dimension_semantics=sem, vmem_limit_bytes=VMEM_LIMIT_BYTES)


def _gelu(x):
    return 0.5 * x * (1.0 + lax.erf(x * math.sqrt(0.5)))


def _rmsnorm_kernel(x_ref, g_ref, o_ref):
    x = x_ref[...]
    ms = jnp.mean(x * x, axis=-1, keepdims=True)
    o_ref[...] = (x * lax.rsqrt(ms + EPS) * g_ref[...]).astype(o_ref.dtype)


def _rmsnorm(x, g, tm):
    s, d = x.shape
    return pl.pallas_call(
        _rmsnorm_kernel,
        grid=(s // tm,),
        in_specs=[pl.BlockSpec((tm, d), lambda i: (i, 0)),
                  pl.BlockSpec((1, d), lambda i: (0, 0))],
        out_specs=pl.BlockSpec((tm, d), lambda i: (i, 0)),
        out_shape=jax.ShapeDtypeStruct((s, d), BF16),
        compiler_params=_params("arbitrary"),
        name="rmsnorm",
    )(x, g.reshape(1, d))


def _sgu_kernel(xn_ref, w_ref, lng_ref, lnb_ref, ws_ref, bs_ref, o_ref, u_scr):
    j = pl.program_id(1)
    act = _gelu(jnp.dot(xn_ref[...], w_ref[...], preferred_element_type=F32))

    @pl.when(j == 0)
    def _():
        u_scr[...] = act

    @pl.when(j == 1)
    def _():
        mu = jnp.mean(act, axis=-1, keepdims=True)
        cen = act - mu
        var = jnp.mean(cen * cen, axis=-1, keepdims=True)
        vln = (cen * lax.rsqrt(var + EPS) * lng_ref[...] + lnb_ref[...]).astype(BF16)
        t_idx = lax.broadcasted_iota(jnp.int32, (CHUNK, CHUNK), 0)
        s_idx = lax.broadcasted_iota(jnp.int32, (CHUNK, CHUNK), 1)
        causal = s_idx <= t_idx
        for g in range(SGU_GROUPS):
            wm = jnp.where(causal, ws_ref[g], 0.0).astype(BF16)
            cols = slice(g * SGU_GROUP_DIM, (g + 1) * SGU_GROUP_DIM)
            for c in range(o_ref.shape[0] // CHUNK):
                rows = slice(c * CHUNK, (c + 1) * CHUNK)
                mixed = jnp.dot(wm, vln[rows, cols], preferred_element_type=F32) + bs_ref[g]
                o_ref[rows, cols] = (u_scr[rows, cols] * mixed).astype(o_ref.dtype)


def _sgu(xn, w_uv, ln_g, ln_b, w_s, b_s, tm):
    s, d = xn.shape
    width = w_uv.shape[1] // 2
    return pl.pallas_call(
        _sgu_kernel,
        grid=(s // tm, 2),
        in_specs=[pl.BlockSpec((tm, d), lambda i, j: (i, 0)),
                  pl.BlockSpec((d, width), lambda i, j: (0, j)),
                  pl.BlockSpec((1, width), lambda i, j: (0, 0)),
                  pl.BlockSpec((1, width), lambda i, j: (0, 0)),
                  pl.BlockSpec(w_s.shape, lambda i, j: (0, 0, 0)),
                  pl.BlockSpec((SGU_GROUPS, CHUNK, 1), lambda i, j: (0, 0, 0))],
        out_specs=pl.BlockSpec((tm, width), lambda i, j: (i, 0)),
        out_shape=jax.ShapeDtypeStruct((s, width), BF16),
        scratch_shapes=[pltpu.VMEM((tm, width), F32)],
        compiler_params=_params("arbitrary", "arbitrary"),
        name="sgu",
    )(xn, w_uv, ln_g.reshape(1, width), ln_b.reshape(1, width), w_s,
      b_s.reshape(SGU_GROUPS, CHUNK, 1))


def _proj_kernel(xn_ref, w_ref, o_ref):
    o_ref[...] = jnp.dot(xn_ref[...], w_ref[...], preferred_element_type=F32).astype(o_ref.dtype)


def _gate_kernel(xn_ref, w_ref, b_ref, o_ref):
    acc = jnp.dot(xn_ref[...], w_ref[...], preferred_element_type=F32)
    o_ref[...] = jax.nn.sigmoid(acc + b_ref[...]).astype(o_ref.dtype)


def _proj(xn, w, tm, tn, out_dtype, bias=None):
    s, d = xn.shape
    n = w.shape[1]
    in_specs = [pl.BlockSpec((tm, d), lambda i, j: (i, 0)),
                pl.BlockSpec((d, tn), lambda i, j: (0, j))]
    args = [xn, w]
    kern = _proj_kernel
    if bias is not None:
        in_specs.append(pl.BlockSpec((1, tn), lambda i, j: (0, j)))
        args.append(bias.reshape(1, n))
        kern = _gate_kernel
    return pl.pallas_call(
        kern,
        grid=(s // tm, n // tn),
        in_specs=in_specs,
        out_specs=pl.BlockSpec((tm, tn), lambda i, j: (i, j)),
        out_shape=jax.ShapeDtypeStruct((s, n), out_dtype),
        compiler_params=_params("arbitrary", "arbitrary"),
        name="gates" if bias is not None else "qkv",
    )(*args)


def _attn_kernel(q_ref, k_ref, v_ref, o_ref, acc_ref, carry_ref):
    t = q_ref.shape[0]
    qi = pl.program_id(1)
    scale = 1.0 / math.sqrt(q_ref.shape[1])
    q = q_ref[...]
    j_idx = lax.broadcasted_iota(jnp.int32, (t, t), 0)
    s_idx = lax.broadcasted_iota(jnp.int32, (t, t), 1)
    later = (j_idx > s_idx).astype(BF16)

    def scores(kb):
        kk = k_ref[pl.ds(kb * t, t), :]
        z = lax.dot_general(q, kk, (((1,), (1,)), ((), ())), preferred_element_type=F32) * scale
        log_beta = jnp.minimum(z, 0.0) - jnp.log1p(jnp.exp(-jnp.abs(z)))
        return log_beta, log_beta - z

    def suffix_sum(log_1mb):
        hi = log_1mb.astype(BF16)
        lo = (log_1mb - hi.astype(F32)).astype(BF16)
        return (jnp.dot(hi, later, preferred_element_type=F32)
                + jnp.dot(lo, later, preferred_element_type=F32))

    log_beta, log_1mb = scores(qi)
    causal = s_idx < j_idx
    log_1mb = jnp.where(causal, log_1mb, 0.0)
    w = jnp.where(causal, jnp.exp(log_beta + suffix_sum(log_1mb)), 0.0)
    acc_ref[...] = jnp.dot(w.astype(BF16), v_ref[pl.ds(qi * t, t), :], preferred_element_type=F32)
    row_sum = jnp.sum(log_1mb, axis=-1, keepdims=True)
    carry_ref[...] = row_sum

    def cond(state):
        kb, top = state
        return jnp.logical_and(kb >= 0, top >= LOG_F32_ZERO)

    def body(state):
        kb, _ = state
        log_beta, log_1mb = scores(kb)
        carry = carry_ref[...]
        w = jnp.exp(log_beta + suffix_sum(log_1mb) + carry)
        acc_ref[...] += jnp.dot(w.astype(BF16), v_ref[pl.ds(kb * t, t), :],
                                preferred_element_type=F32)
        carry = carry + jnp.sum(log_1mb, axis=-1, keepdims=True)
        carry_ref[...] = carry
        return kb - 1, jnp.max(carry)

    lax.while_loop(cond, body, (qi - 1, jnp.max(row_sum)))
    o_ref[...] = acc_ref[...].astype(o_ref.dtype)


def _attention(qkv):
    s = qkv.shape[0]
    t = ATT_BLOCK
    dh = SB_HEAD_DIM
    return pl.pallas_call(
        _attn_kernel,
        grid=(SB_HEADS, s // t),
        in_specs=[pl.BlockSpec((t, dh), lambda h, i: (i, h)),
                  pl.BlockSpec((s, dh), lambda h, i: (0, SB_HEADS + h)),
                  pl.BlockSpec((s, dh), lambda h, i: (0, 2 * SB_HEADS + h))],
        out_specs=pl.BlockSpec((t, dh), lambda h, i: (i, h)),
        out_shape=jax.ShapeDtypeStruct((s, SB_HEADS * dh), BF16),
        scratch_shapes=[pltpu.VMEM((t, dh), F32), pltpu.VMEM((t, 1), F32)],
        compiler_params=_params("arbitrary", "arbitrary"),
        name="sb_attention",
    )(qkv, qkv, qkv)


def _merge_kernel(a_ref, b_ref, g_ref, x_ref, wa_ref, wb_ref, wo_ref, o_ref):
    d = x_ref.shape[1]
    y_a = jnp.dot(a_ref[...], wa_ref[...], preferred_element_type=F32)
    y_b = jnp.dot(b_ref[...], wb_ref[...], preferred_element_type=F32)
    m = g_ref[:, :d] * y_a + g_ref[:, d:] * y_b
    o_ref[...] = x_ref[...] + jnp.dot(m.astype(BF16), wo_ref[...], preferred_element_type=F32)


def _merge(a, b, gates, x, w_a, w_b, w_o, tm):
    s, d = x.shape
    wa = a.shape[1]
    wb = b.shape[1]
    return pl.pallas_call(
        _merge_kernel,
        grid=(s // tm,),
        in_specs=[pl.BlockSpec((tm, wa), lambda i: (i, 0)),
                  pl.BlockSpec((tm, wb), lambda i: (i, 0)),
                  pl.BlockSpec((tm, 2 * d), lambda i: (i, 0)),
                  pl.BlockSpec((tm, d), lambda i: (i, 0)),
                  pl.BlockSpec((wa, d), lambda i: (0, 0)),
                  pl.BlockSpec((wb, d), lambda i: (0, 0)),
                  pl.BlockSpec((d, d), lambda i: (0, 0))],
        out_specs=pl.BlockSpec((tm, d), lambda i: (i, 0)),
        out_shape=jax.ShapeDtypeStruct((s, d), F32),
        compiler_params=_params("arbitrary"),
        name="merge",
    )(a, b, gates, x, w_a, w_b, w_o)


def _ffn_kernel(x_ref, gn_ref, wg_ref, wu_ref, wd_ref, gf_ref, o_ref, hn_scr, *, final_norm):
    f = pl.program_id(1)

    @pl.when(f == 0)
    def _():
        x = x_ref[...]
        ms = jnp.mean(x * x, axis=-1, keepdims=True)
        hn_scr[...] = (x * lax.rsqrt(ms + EPS) * gn_ref[...]).astype(BF16)
        o_ref[...] = x

    hn = hn_scr[...]
    gate = jnp.dot(hn, wg_ref[...], preferred_element_type=F32)
    up = jnp.dot(hn, wu_ref[...], preferred_element_type=F32)
    h = (gate * jax.nn.sigmoid(gate) * up).astype(BF16)
    o_ref[...] += jnp.dot(h, wd_ref[...], preferred_element_type=F32)

    if final_norm:
        @pl.when(f == pl.num_programs(1) - 1)
        def _():
            y = o_ref[...]
            ms = jnp.mean(y * y, axis=-1, keepdims=True)
            o_ref[...] = y * lax.rsqrt(ms + EPS) * gf_ref[...]


def _ffn(x, g_ffn, w_gate_up, w_down, g_final, tm, tf, final_norm):
    s, d = x.shape
    d_ff = w_down.shape[0]
    nf = d_ff // tf
    return pl.pallas_call(
        functools.partial(_ffn_kernel, final_norm=final_norm),
        grid=(s // tm, nf),
        in_specs=[pl.BlockSpec((tm, d), lambda i, f: (i, 0)),
                  pl.BlockSpec((1, d), lambda i, f: (0, 0)),
                  pl.BlockSpec((d, tf), lambda i, f: (0, f)),
                  pl.BlockSpec((d, tf), lambda i, f: (0, nf + f)),
                  pl.BlockSpec((tf, d), lambda i, f: (f, 0)),
                  pl.BlockSpec((1, d), lambda i, f: (0, 0))],
        out_specs=pl.BlockSpec((tm, d), lambda i, f: (i, 0)),
        out_shape=jax.ShapeDtypeStruct((s, d), F32),
        scratch_shapes=[pltpu.VMEM((tm, d), BF16)],
        compiler_params=_params("arbitrary", "arbitrary"),
        name="ffn",
    )(x, g_ffn.reshape(1, d), w_gate_up, w_gate_up, w_down, g_final.reshape(1, d))


def _pick(n, pref):
    return pref if n % pref == 0 else n


def kernel(x, norm_mix, w_in, b_gate, ln_v_g, ln_v_b, w_s, b_s, w_proj_a, w_proj_b,
           w_out, norm_ffn, w_gate_up, w_down, norm_final):
    batch, seq, d = x.shape
    depth = w_in.shape[0]
    sgu_w = SGU_GROUPS * SGU_GROUP_DIM
    sb_w = SB_HEADS * SB_HEAD_DIM
    off_q = 2 * sgu_w
    off_g = off_q + 3 * sb_w
    outs = []
    for b in range(batch):
        h = x.reshape(seq, d) if batch == 1 else x[b]
        for l in range(depth):
            w_in_l = w_in[l].astype(BF16)
            xn = _rmsnorm(h, norm_mix[l], _pick(seq, 512))
            a = _sgu(xn, w_in_l[:, :off_q], ln_v_g[l], ln_v_b[l], w_s[l], b_s[l], _pick(seq, 512))
            qkv = _proj(xn, w_in_l[:, off_q:off_g], _pick(seq, 1024), sb_w, BF16)
            gates = _proj(xn, w_in_l[:, off_g:], _pick(seq, 1024), _pick(2 * d, 1024), F32,
                          bias=b_gate[l])
            o_b = _attention(qkv)
            h = _merge(a, o_b, gates, h, w_proj_a[l].astype(BF16), w_proj_b[l].astype(BF16),
                       w_out[l].astype(BF16), _pick(seq, 256))
            h = _ffn(h, norm_ffn[l], w_gate_up[l].astype(BF16), w_down[l].astype(BF16),
                     norm_final, _pick(seq, 512), _pick(w_down.shape[1], 512),
                     final_norm=(l == depth - 1))
        outs.append(h)
    return outs[0].reshape(x.shape) if batch == 1 else jnp.stack(outs)
```

```python
import functools
import math

import jax
import jax.numpy as jnp
from jax import lax
from jax.experimental import pallas as pl
from jax.experimental.pallas import tpu as pltpu

EPS = 1e-6
CHUNK = 128
SGU_GROUPS = 8
SGU_GROUP_DIM = 128
SB_HEADS = 8
SB_HEAD_DIM = 128
ATT_BLOCK = 256
STICK_USED_F32_ZERO = 104.0
ATT_HEADS_PER_STEP = 4
VMEM_LIMIT_BYTES = 56 * 1024 * 1024

F32 = jnp.float32
BF16 = jnp.bfloat16


def _params(*sem):
    return pltpu.CompilerParams(dimension_semantics=sem, vmem_limit_bytes=VMEM_LIMIT_BYTES)


def _gelu(x):
    return 0.5 * x * (1.0 + lax.erf(x * math.sqrt(0.5)))


def _rmsnorm_kernel(x_ref, g_ref, o_ref):
    x = x_ref[...]
    ms = jnp.mean(x * x, axis=-1, keepdims=True)
    o_ref[...] = (x * lax.rsqrt(ms + EPS) * g_ref[...]).astype(o_ref.dtype)


def _rmsnorm(x, g, tm):
    s, d = x.shape
    return pl.pallas_call(
        _rmsnorm_kernel,
        grid=(s // tm,),
        in_specs=[pl.BlockSpec((tm, d), lambda i: (i, 0)),
                  pl.BlockSpec((1, d), lambda i: (0, 0))],
        out_specs=pl.BlockSpec((tm, d), lambda i: (i, 0)),
        out_shape=jax.ShapeDtypeStruct((s, d), BF16),
        compiler_params=_params("arbitrary"),
        name="rmsnorm",
    )(x, g.reshape(1, d))


def _sgu_kernel(xn_ref, w_ref, lng_ref, lnb_ref, ws_ref, bs_ref, o_ref):
    width = o_ref.shape[1]
    xn = xn_ref[...]
    v = _gelu(jnp.dot(xn, w_ref[:, width:], preferred_element_type=F32))
    mu = jnp.mean(v, axis=-1, keepdims=True)
    cen = v - mu
    var = jnp.mean(cen * cen, axis=-1, keepdims=True)
    vln = (cen * lax.rsqrt(var + EPS) * lng_ref[...] + lnb_ref[...]).astype(BF16)
    u = _gelu(jnp.dot(xn, w_ref[:, :width], preferred_element_type=F32))
    t_idx = lax.broadcasted_iota(jnp.int32, (CHUNK, CHUNK), 0)
    s_idx = lax.broadcasted_iota(jnp.int32, (CHUNK, CHUNK), 1)
    causal = s_idx <= t_idx
    for g in range(SGU_GROUPS):
        wm = jnp.where(causal, ws_ref[g], 0.0).astype(BF16)
        cols = slice(g * SGU_GROUP_DIM, (g + 1) * SGU_GROUP_DIM)
        for c in range(o_ref.shape[0] // CHUNK):
            rows = slice(c * CHUNK, (c + 1) * CHUNK)
            mixed = jnp.dot(wm, vln[rows, cols], preferred_element_type=F32) + bs_ref[g]
            o_ref[rows, cols] = (u[rows, cols] * mixed).astype(o_ref.dtype)


def _sgu(xn, w_in, ln_g, ln_b, w_s, b_s, tm):
    s, d = xn.shape
    width = SGU_GROUPS * SGU_GROUP_DIM
    return pl.pallas_call(
        _sgu_kernel,
        grid=(s // tm,),
        in_specs=[pl.BlockSpec((tm, d), lambda i: (i, 0)),
                  pl.BlockSpec((d, 2 * width), lambda i: (0, 0), pipeline_mode=pl.Buffered(1)),
                  pl.BlockSpec((1, width), lambda i: (0, 0)),
                  pl.BlockSpec((1, width), lambda i: (0, 0)),
                  pl.BlockSpec(w_s.shape, lambda i: (0, 0, 0)),
                  pl.BlockSpec((SGU_GROUPS, CHUNK, 1), lambda i: (0, 0, 0))],
        out_specs=pl.BlockSpec((tm, width), lambda i: (i, 0)),
        out_shape=jax.ShapeDtypeStruct((s, width), BF16),
        compiler_params=_params("arbitrary"),
        name="sgu",
    )(xn, w_in, ln_g.reshape(1, width), ln_b.reshape(1, width), w_s,
      b_s.reshape(SGU_GROUPS, CHUNK, 1))


def _proj_kernel(xn_ref, w_ref, o_ref):
    o_ref[...] = jnp.dot(xn_ref[...], w_ref[...], preferred_element_type=F32).astype(o_ref.dtype)


def _gate_kernel(xn_ref, w_ref, b_ref, o_ref):
    acc = jnp.dot(xn_ref[...], w_ref[...], preferred_element_type=F32)
    o_ref[...] = jax.nn.sigmoid(acc + b_ref[...]).astype(o_ref.dtype)


def _proj(xn, w_in, col0, n, tm, tn, out_dtype, bias=None):
    s, d = xn.shape
    j0 = col0 // tn
    in_specs = [pl.BlockSpec((tm, d), lambda i, j: (i, 0)),
                pl.BlockSpec((d, tn), lambda i, j: (0, j0 + j))]
    args = [xn, w_in]
    kern = _proj_kernel
    if bias is not None:
        in_specs.append(pl.BlockSpec((1, tn), lambda i, j: (0, j)))
        args.append(bias.reshape(1, n))
        kern = _gate_kernel
    return pl.pallas_call(
        kern,
        grid=(s // tm, n // tn),
        in_specs=in_specs,
        out_specs=pl.BlockSpec((tm, tn), lambda i, j: (i, j)),
        out_shape=jax.ShapeDtypeStruct((s, n), out_dtype),
        compiler_params=_params("arbitrary", "arbitrary"),
        name="gates" if bias is not None else "qkv",
    )(*args)


def _attn_kernel(q_ref, k_ref, v_ref, o_ref, acc_ref, used_ref, *, heads):
    t = q_ref.shape[0]
    dh = SB_HEAD_DIM
    qi = pl.program_id(1)
    scale = 1.0 / math.sqrt(dh)
    row = lax.broadcasted_iota(jnp.int32, (t, t), 0)
    col = lax.broadcasted_iota(jnp.int32, (t, t), 1)
    later = (row > col).astype(BF16)
    later2 = jnp.concatenate([later, later], axis=0)
    causal = col < row
    has_prev = qi > 0
    kb_prev = jnp.maximum(qi - 1, 0)

    def block(h, kb, mask):
        cols = slice(h * dh, (h + 1) * dh)
        kk = k_ref[pl.ds(kb * t, t), cols]
        z = lax.dot_general(q_ref[:, cols], kk, (((1,), (1,)), ((), ())),
                            preferred_element_type=F32) * scale
        e = jnp.exp(-jnp.abs(z))
        log1p_e = jnp.where(e < 2.0 ** -12, e, jnp.log(1.0 + e))
        used = jnp.maximum(z, 0.0) + log1p_e
        log_beta = z - used
        if mask is not None:
            used = jnp.where(mask, used, 0.0)
        hi = used.astype(BF16)
        lo = (used - hi.astype(F32)).astype(BF16)
        after = jnp.dot(jnp.concatenate([hi, lo], axis=1), later2, preferred_element_type=F32)
        return log_beta - after, jnp.sum(used, axis=-1, keepdims=True)

    def pv(h, kb, w):
        cols = slice(h * dh, (h + 1) * dh)
        return jnp.dot(w.astype(BF16), v_ref[pl.ds(kb * t, t), cols], preferred_element_type=F32)

    low = None
    for h in range(heads):
        cols = slice(h * dh, (h + 1) * dh)
        logw_d, used_d = block(h, qi, causal)
        acc = pv(h, qi, jnp.where(causal, jnp.exp(logw_d), 0.0))
        logw_p, used_p = block(h, kb_prev, None)
        acc_p = pv(h, kb_prev, jnp.exp(logw_p - used_d))
        acc_ref[:, cols] = acc + jnp.where(has_prev, acc_p, 0.0)
        used = used_d + used_p
        used_ref[h] = used
        low = jnp.min(used) if low is None else jnp.minimum(low, jnp.min(used))

    def cond(state):
        kb, low = state
        return jnp.logical_and(kb >= 0, low <= STICK_USED_F32_ZERO)

    def body(state):
        kb, _ = state
        low = None
        for h in range(heads):
            cols = slice(h * dh, (h + 1) * dh)
            logw, used_b = block(h, kb, None)
            used = used_ref[h]
            acc_ref[:, cols] += pv(h, kb, jnp.exp(logw - used))
            used = used + used_b
            used_ref[h] = used
            low = jnp.min(used) if low is None else jnp.minimum(low, jnp.min(used))
        return kb - 1, low

    lax.while_loop(cond, body, (qi - 2, low))
    o_ref[...] = acc_ref[...].astype(o_ref.dtype)


def _attention(qkv, heads):
    s = qkv.shape[0]
    t = ATT_BLOCK
    w = heads * SB_HEAD_DIM
    groups = SB_HEADS // heads
    resident = dict(pipeline_mode=pl.Buffered(1))
    return pl.pallas_call(
        functools.partial(_attn_kernel, heads=heads),
        grid=(groups, s // t),
        in_specs=[pl.BlockSpec((t, w), lambda g, i: (i, g)),
                  pl.BlockSpec((s, w), lambda g, i: (0, groups + g), **resident),
                  pl.BlockSpec((s, w), lambda g, i: (0, 2 * groups + g), **resident)],
        out_specs=pl.BlockSpec((t, w), lambda g, i: (i, g)),
        out_shape=jax.ShapeDtypeStruct((s, SB_HEADS * SB_HEAD_DIM), BF16),
        scratch_shapes=[pltpu.VMEM((t, w), F32), pltpu.VMEM((heads, t, 1), F32)],
        compiler_params=_params("arbitrary", "arbitrary"),
        name="sb_attention",
    )(qkv, qkv, qkv)


def _merge_kernel(a_ref, b_ref, g_ref, x_ref, wa_ref, wb_ref, wo_ref, o_ref):
    d = x_ref.shape[1]
    y_a = jnp.dot(a_ref[...], wa_ref[...], preferred_element_type=F32)
    y_b = jnp.dot(b_ref[...], wb_ref[...], preferred_element_type=F32)
    m = g_ref[:, :d] * y_a + g_ref[:, d:] * y_b
    o_ref[...] = x_ref[...] + jnp.dot(m.astype(BF16), wo_ref[...], preferred_element_type=F32)


def _merge(a, b, gates, x, w_a, w_b, w_o, tm):
    s, d = x.shape
    wa = a.shape[1]
    wb = b.shape[1]
    return pl.pallas_call(
        _merge_kernel,
        grid=(s // tm,),
        in_specs=[pl.BlockSpec((tm, wa), lambda i: (i, 0)),
                  pl.BlockSpec((tm, wb), lambda i: (i, 0)),
                  pl.BlockSpec((tm, 2 * d), lambda i: (i, 0)),
                  pl.BlockSpec((tm, d), lambda i: (i, 0)),
                  pl.BlockSpec((wa, d), lambda i: (0, 0)),
                  pl.BlockSpec((wb, d), lambda i: (0, 0)),
                  pl.BlockSpec((d, d), lambda i: (0, 0))],
        out_specs=pl.BlockSpec((tm, d), lambda i: (i, 0)),
        out_shape=jax.ShapeDtypeStruct((s, d), F32),
        compiler_params=_params("arbitrary"),
        name="merge",
    )(a, b, gates, x, w_a, w_b, w_o)


def _ffn_kernel(x_ref, gn_ref, wg_ref, wu_ref, wd_ref, gf_ref, o_ref, hn_scr, *, final_norm):
    f = pl.program_id(1)

    @pl.when(f == 0)
    def _():
        x = x_ref[...]
        ms = jnp.mean(x * x, axis=-1, keepdims=True)
        hn_scr[...] = (x * lax.rsqrt(ms + EPS) * gn_ref[...]).astype(BF16)
        o_ref[...] = x

    hn = hn_scr[...]
    gate = jnp.dot(hn, wg_ref[...], preferred_element_type=F32)
    up = jnp.dot(hn, wu_ref[...], preferred_element_type=F32)
    h = (gate * jax.nn.sigmoid(gate) * up).astype(BF16)
    o_ref[...] += jnp.dot(h, wd_ref[...], preferred_element_type=F32)

    if final_norm:
        @pl.when(f == pl.num_programs(1) - 1)
        def _():
            y = o_ref[...]
            ms = jnp.mean(y * y, axis=-1, keepdims=True)
            o_ref[...] = y * lax.rsqrt(ms + EPS) * gf_ref[...]


def _ffn(x, g_ffn, w_gate_up, w_down, g_final, tm, tf, final_norm):
    s, d = x.shape
    d_ff = w_down.shape[0]
    nf = d_ff // tf
    return pl.pallas_call(
        functools.partial(_ffn_kernel, final_norm=final_norm),
        grid=(s // tm, nf),
        in_specs=[pl.BlockSpec((tm, d), lambda i, f: (i, 0)),
                  pl.BlockSpec((1, d), lambda i, f: (0, 0)),
                  pl.BlockSpec((d, tf), lambda i, f: (0, f)),
                  pl.BlockSpec((d, tf), lambda i, f: (0, nf + f)),
                  pl.BlockSpec((tf, d), lambda i, f: (f, 0)),
                  pl.BlockSpec((1, d), lambda i, f: (0, 0))],
        out_specs=pl.BlockSpec((tm, d), lambda i, f: (i, 0)),
        out_shape=jax.ShapeDtypeStruct((s, d), F32),
        scratch_shapes=[pltpu.VMEM((tm, d), BF16)],
        compiler_params=_params("arbitrary", "arbitrary"),
        name="ffn",
    )(x, g_ffn.reshape(1, d), w_gate_up, w_gate_up, w_down, g_final.reshape(1, d))


def _pick(n, pref):
    return pref if n % pref == 0 else n


def kernel(x, norm_mix, w_in, b_gate, ln_v_g, ln_v_b, w_s, b_s, w_proj_a, w_proj_b,
           w_out, norm_ffn, w_gate_up, w_down, norm_final):
    batch, seq, d = x.shape
    depth = w_in.shape[0]
    sgu_w = SGU_GROUPS * SGU_GROUP_DIM
    sb_w = SB_HEADS * SB_HEAD_DIM
    off_q = 2 * sgu_w
    off_g = off_q + 3 * sb_w
    outs = []
    for b in range(batch):
        h = x.reshape(seq, d) if batch == 1 else x[b]
        for l in range(depth):
            w_in_l = w_in[l].astype(BF16)
            xn = _rmsnorm(h, norm_mix[l], _pick(seq, 512))
            a = _sgu(xn, w_in_l, ln_v_g[l], ln_v_b[l], w_s[l], b_s[l], _pick(seq, 512))
            qkv = _proj(xn, w_in_l, off_q, 3 * sb_w, _pick(seq, 1024), sb_w, BF16)
            gates = _proj(xn, w_in_l, off_g, 2 * d, _pick(seq, 1024), _pick(2 * d, 1024), F32,
                          bias=b_gate[l])
            o_b = _attention(qkv, ATT_HEADS_PER_STEP)
            h = _merge(a, o_b, gates, h, w_proj_a[l].astype(BF16), w_proj_b[l].astype(BF16),
                       w_out[l].astype(BF16), _pick(seq, 256))
            h = _ffn(h, norm_ffn[l], w_gate_up[l].astype(BF16), w_down[l].astype(BF16),
                     norm_final, _pick(seq, 512), _pick(w_down.shape[1], 512),
                     final_norm=(l == depth - 1))
        outs.append(h)
    return outs[0].reshape(x.shape) if batch == 1 else jnp.stack(outs)
```

```python
import functools
import math

import jax
import jax.numpy as jnp
from jax import lax
from jax.experimental import pallas as pl
from jax.experimental.pallas import tpu as pltpu

EPS = 1e-6
CHUNK = 128
SGU_GROUPS = 8
SGU_GROUP_DIM = 128
SB_HEADS = 8
SB_HEAD_DIM = 128
ATT_BLOCK = 256
STICK_USED_F32_ZERO = 104.0
ATT_HEADS_PER_STEP = 4
VMEM_LIMIT_BYTES = 60 * 1024 * 1024

F32 = jnp.float32
BF16 = jnp.bfloat16


def _params(*sem):
    return pltpu.CompilerParams(dimension_semantics=sem, vmem_limit_bytes=VMEM_LIMIT_BYTES)


def _gelu(x):
    return 0.5 * x * (1.0 + lax.erf(x * math.sqrt(0.5)))


def _sgu_kernel(x_ref, gn_ref, w_ref, lng_ref, lnb_ref, ws_ref, bs_ref, xn_ref, o_ref):
    width = o_ref.shape[1]
    x = x_ref[...]
    ms = jnp.mean(x * x, axis=-1, keepdims=True)
    xn = (x * lax.rsqrt(ms + EPS) * gn_ref[...]).astype(BF16)
    xn_ref[...] = xn
    v = _gelu(jnp.dot(xn, w_ref[:, width:], preferred_element_type=F32))
    mu = jnp.mean(v, axis=-1, keepdims=True)
    cen = v - mu
    var = jnp.mean(cen * cen, axis=-1, keepdims=True)
    vln = (cen * lax.rsqrt(var + EPS) * lng_ref[...] + lnb_ref[...]).astype(BF16)
    u = _gelu(jnp.dot(xn, w_ref[:, :width], preferred_element_type=F32))
    t_idx = lax.broadcasted_iota(jnp.int32, (CHUNK, CHUNK), 0)
    s_idx = lax.broadcasted_iota(jnp.int32, (CHUNK, CHUNK), 1)
    causal = s_idx <= t_idx
    for g in range(SGU_GROUPS):
        wm = jnp.where(causal, ws_ref[g], 0.0).astype(BF16)
        cols = slice(g * SGU_GROUP_DIM, (g + 1) * SGU_GROUP_DIM)
        for c in range(o_ref.shape[0] // CHUNK):
            rows = slice(c * CHUNK, (c + 1) * CHUNK)
            mixed = jnp.dot(wm, vln[rows, cols], preferred_element_type=F32) + bs_ref[g]
            o_ref[rows, cols] = (u[rows, cols] * mixed).astype(o_ref.dtype)


def _sgu(x, g_norm, w_in, ln_g, ln_b, w_s, b_s, tm):
    s, d = x.shape
    width = SGU_GROUPS * SGU_GROUP_DIM
    return pl.pallas_call(
        _sgu_kernel,
        grid=(s // tm,),
        in_specs=[pl.BlockSpec((tm, d), lambda i: (i, 0)),
                  pl.BlockSpec((1, d), lambda i: (0, 0)),
                  pl.BlockSpec((d, 2 * width), lambda i: (0, 0), pipeline_mode=pl.Buffered(1)),
                  pl.BlockSpec((1, width), lambda i: (0, 0)),
                  pl.BlockSpec((1, width), lambda i: (0, 0)),
                  pl.BlockSpec(w_s.shape, lambda i: (0, 0, 0)),
                  pl.BlockSpec((SGU_GROUPS, CHUNK, 1), lambda i: (0, 0, 0))],
        out_specs=[pl.BlockSpec((tm, d), lambda i: (i, 0)),
                   pl.BlockSpec((tm, width), lambda i: (i, 0))],
        out_shape=[jax.ShapeDtypeStruct((s, d), BF16),
                   jax.ShapeDtypeStruct((s, width), BF16)],
        compiler_params=_params("arbitrary"),
        name="sgu",
    )(x, g_norm.reshape(1, d), w_in, ln_g.reshape(1, width), ln_b.reshape(1, width), w_s,
      b_s.reshape(SGU_GROUPS, CHUNK, 1))


def _proj_kernel(xn_ref, *refs, gate):
    o_ref = refs[-1]
    w_refs = refs[:-2] if gate else refs[:-1]
    tn = w_refs[0].shape[1]
    xn = xn_ref[...]
    for k, w_ref in enumerate(w_refs):
        cols = slice(k * tn, (k + 1) * tn)
        acc = jnp.dot(xn, w_ref[...], preferred_element_type=F32)
        if gate:
            acc = jax.nn.sigmoid(acc + refs[-2][:, cols])
        o_ref[:, cols] = acc.astype(o_ref.dtype)


def _proj(xn, w_in, col0, n, tm, tn, out_dtype, bias=None):
    s, d = xn.shape
    j0 = col0 // tn
    in_specs = [pl.BlockSpec((tm, d), lambda i: (i, 0))]
    args = [xn]
    for k in range(n // tn):
        in_specs.append(pl.BlockSpec((d, tn), lambda i, k=k: (0, j0 + k),
                                     pipeline_mode=pl.Buffered(1)))
        args.append(w_in)
    if bias is not None:
        in_specs.append(pl.BlockSpec((1, n), lambda i: (0, 0)))
        args.append(bias.reshape(1, n))
    return pl.pallas_call(
        functools.partial(_proj_kernel, gate=bias is not None),
        grid=(s // tm,),
        in_specs=in_specs,
        out_specs=pl.BlockSpec((tm, n), lambda i: (i, 0)),
        out_shape=jax.ShapeDtypeStruct((s, n), out_dtype),
        compiler_params=_params("arbitrary"),
        name="gates" if bias is not None else "qkv",
    )(*args)


def _attn_kernel(q_ref, k_ref, v_ref, o_ref, acc_ref, used_ref, *, heads):
    t = q_ref.shape[0]
    dh = SB_HEAD_DIM
    qi = pl.program_id(1)
    scale = 1.0 / math.sqrt(dh)
    row = lax.broadcasted_iota(jnp.int32, (t, t), 0)
    col = lax.broadcasted_iota(jnp.int32, (t, t), 1)
    later = (row > col).astype(BF16)
    later2 = jnp.concatenate([later, later], axis=0)
    causal = col < row
    has_prev = qi > 0
    kb_prev = jnp.maximum(qi - 1, 0)

    def block(h, kb, mask):
        cols = slice(h * dh, (h + 1) * dh)
        kk = k_ref[pl.ds(kb * t, t), cols]
        z = lax.dot_general(q_ref[:, cols], kk, (((1,), (1,)), ((), ())),
                            preferred_element_type=F32) * scale
        e = jnp.exp(-jnp.abs(z))
        log1p_e = jnp.where(e < 2.0 ** -12, e, jnp.log(1.0 + e))
        used = jnp.maximum(z, 0.0) + log1p_e
        log_beta = z - used
        if mask is not None:
            used = jnp.where(mask, used, 0.0)
        hi = used.astype(BF16)
        lo = (used - hi.astype(F32)).astype(BF16)
        after = jnp.dot(jnp.concatenate([hi, lo], axis=1), later2, preferred_element_type=F32)
        return log_beta - after, jnp.sum(used, axis=-1, keepdims=True)

    def pv(h, kb, w):
        cols = slice(h * dh, (h + 1) * dh)
        return jnp.dot(w.astype(BF16), v_ref[pl.ds(kb * t, t), cols], preferred_element_type=F32)

    low = None
    for h in range(heads):
        cols = slice(h * dh, (h + 1) * dh)
        logw_d, used_d = block(h, qi, causal)
        acc = pv(h, qi, jnp.where(causal, jnp.exp(logw_d), 0.0))
        logw_p, used_p = block(h, kb_prev, None)
        acc_p = pv(h, kb_prev, jnp.exp(logw_p - used_d))
        acc_ref[:, cols] = acc + jnp.where(has_prev, acc_p, 0.0)
        used = used_d + used_p
        used_ref[h] = used
        low = jnp.min(used) if low is None else jnp.minimum(low, jnp.min(used))

    def cond(state):
        kb, low = state
        return jnp.logical_and(kb >= 0, low <= STICK_USED_F32_ZERO)

    def body(state):
        kb, _ = state
        low = None
        for h in range(heads):
            cols = slice(h * dh, (h + 1) * dh)
            logw, used_b = block(h, kb, None)
            used = used_ref[h]
            acc_ref[:, cols] += pv(h, kb, jnp.exp(logw - used))
            used = used + used_b
            used_ref[h] = used
            low = jnp.min(used) if low is None else jnp.minimum(low, jnp.min(used))
        return kb - 1, low

    lax.while_loop(cond, body, (qi - 2, low))
    o_ref[...] = acc_ref[...].astype(o_ref.dtype)


def _attention(qkv, heads):
    s = qkv.shape[0]
    t = ATT_BLOCK
    w = heads * SB_HEAD_DIM
    groups = SB_HEADS // heads
    resident = dict(pipeline_mode=pl.Buffered(1))
    return pl.pallas_call(
        functools.partial(_attn_kernel, heads=heads),
        grid=(groups, s // t),
        in_specs=[pl.BlockSpec((t, w), lambda g, i: (i, g)),
                  pl.BlockSpec((s, w), lambda g, i: (0, groups + g), **resident),
                  pl.BlockSpec((s, w), lambda g, i: (0, 2 * groups + g), **resident)],
        out_specs=pl.BlockSpec((t, w), lambda g, i: (i, g)),
        out_shape=jax.ShapeDtypeStruct((s, SB_HEADS * SB_HEAD_DIM), BF16),
        scratch_shapes=[pltpu.VMEM((t, w), F32), pltpu.VMEM((heads, t, 1), F32)],
        compiler_params=_params("arbitrary", "arbitrary"),
        name="sb_attention",
    )(qkv, qkv, qkv)


def _merge_kernel(a_ref, b_ref, g_ref, x_ref, wa_ref, wb_ref, wo_ref, gn_ref, o_ref, hn_ref):
    d = x_ref.shape[1]
    y_a = jnp.dot(a_ref[...], wa_ref[...], preferred_element_type=F32)
    y_b = jnp.dot(b_ref[...], wb_ref[...], preferred_element_type=F32)
    m = g_ref[:, :d] * y_a + g_ref[:, d:] * y_b
    y = x_ref[...] + jnp.dot(m.astype(BF16), wo_ref[...], preferred_element_type=F32)
    o_ref[...] = y
    ms = jnp.mean(y * y, axis=-1, keepdims=True)
    hn_ref[...] = (y * lax.rsqrt(ms + EPS) * gn_ref[...]).astype(BF16)


def _merge(a, b, gates, x, w_a, w_b, w_o, g_norm, tm):
    s, d = x.shape
    wa = a.shape[1]
    wb = b.shape[1]
    resident = dict(pipeline_mode=pl.Buffered(1))
    return pl.pallas_call(
        _merge_kernel,
        grid=(s // tm,),
        in_specs=[pl.BlockSpec((tm, wa), lambda i: (i, 0)),
                  pl.BlockSpec((tm, wb), lambda i: (i, 0)),
                  pl.BlockSpec((tm, 2 * d), lambda i: (i, 0)),
                  pl.BlockSpec((tm, d), lambda i: (i, 0)),
                  pl.BlockSpec((wa, d), lambda i: (0, 0), **resident),
                  pl.BlockSpec((wb, d), lambda i: (0, 0), **resident),
                  pl.BlockSpec((d, d), lambda i: (0, 0), **resident),
                  pl.BlockSpec((1, d), lambda i: (0, 0))],
        out_specs=[pl.BlockSpec((tm, d), lambda i: (i, 0)),
                   pl.BlockSpec((tm, d), lambda i: (i, 0))],
        out_shape=[jax.ShapeDtypeStruct((s, d), F32),
                   jax.ShapeDtypeStruct((s, d), BF16)],
        compiler_params=_params("arbitrary"),
        name="merge",
    )(a, b, gates, x, w_a, w_b, w_o, g_norm.reshape(1, d))


def _ffn_kernel(hn_ref, x_hbm, wg_ref, wu_ref, wd_ref, gf_ref, o_ref, x_scr, x_sem, *, final_norm):
    i = pl.program_id(0)
    f = pl.program_id(1)
    tm = o_ref.shape[0]
    x_copy = pltpu.make_async_copy(x_hbm.at[pl.ds(pl.multiple_of(i * tm, tm), tm), :], x_scr, x_sem)

    @pl.when(f == 0)
    def _():
        x_copy.start()
        o_ref[...] = jnp.zeros_like(o_ref)

    hn = hn_ref[...]
    gate = jnp.dot(hn, wg_ref[...], preferred_element_type=F32)
    up = jnp.dot(hn, wu_ref[...], preferred_element_type=F32)
    h = (gate * jax.nn.sigmoid(gate) * up).astype(BF16)
    tn = _pick(o_ref.shape[1], wg_ref.shape[1])
    for c in range(o_ref.shape[1] // tn):
        cols = slice(c * tn, (c + 1) * tn)
        o_ref[:, cols] += jnp.dot(h, wd_ref[:, cols], preferred_element_type=F32)

    @pl.when(f == pl.num_programs(1) - 1)
    def _():
        x_copy.wait()
        y = x_scr[...] + o_ref[...]
        if final_norm:
            ms = jnp.mean(y * y, axis=-1, keepdims=True)
            y = y * lax.rsqrt(ms + EPS) * gf_ref[...]
        o_ref[...] = y


def _ffn(hn, x, w_gate_up, w_down, g_final, tm, tf, final_norm):
    s, d = x.shape
    d_ff = w_down.shape[0]
    nf = d_ff // tf
    return pl.pallas_call(
        functools.partial(_ffn_kernel, final_norm=final_norm),
        grid=(s // tm, nf),
        in_specs=[pl.BlockSpec((tm, d), lambda i, f: (i, 0)),
                  pl.BlockSpec(memory_space=pl.ANY),
                  pl.BlockSpec((d, tf), lambda i, f: (0, f)),
                  pl.BlockSpec((d, tf), lambda i, f: (0, nf + f)),
                  pl.BlockSpec((tf, d), lambda i, f: (f, 0)),
                  pl.BlockSpec((1, d), lambda i, f: (0, 0))],
        out_specs=pl.BlockSpec((tm, d), lambda i, f: (i, 0)),
        out_shape=jax.ShapeDtypeStruct((s, d), F32),
        scratch_shapes=[pltpu.VMEM((tm, d), F32), pltpu.SemaphoreType.DMA(())],
        compiler_params=_params("arbitrary", "arbitrary"),
        name="ffn",
    )(hn, x, w_gate_up, w_gate_up, w_down, g_final.reshape(1, d))


def _pick(n, pref):
    return pref if n % pref == 0 else n


def kernel(x, norm_mix, w_in, b_gate, ln_v_g, ln_v_b, w_s, b_s, w_proj_a, w_proj_b,
           w_out, norm_ffn, w_gate_up, w_down, norm_final):
    batch, seq, d = x.shape
    depth = w_in.shape[0]
    sgu_w = SGU_GROUPS * SGU_GROUP_DIM
    sb_w = SB_HEADS * SB_HEAD_DIM
    off_q = 2 * sgu_w
    off_g = off_q + 3 * sb_w
    outs = []
    for b in range(batch):
        h = x.reshape(seq, d) if batch == 1 else x[b]
        for l in range(depth):
            w_in_l = w_in[l].astype(BF16)
            xn, a = _sgu(h, norm_mix[l], w_in_l, ln_v_g[l], ln_v_b[l], w_s[l], b_s[l],
                         _pick(seq, 512))
            qkv = _proj(xn, w_in_l, off_q, 3 * sb_w, _pick(seq, 1024), sb_w, BF16)
            gates = _proj(xn, w_in_l, off_g, 2 * d, _pick(seq, 512), _pick(2 * d, 1024), F32,
                          bias=b_gate[l])
            o_b = _attention(qkv, ATT_HEADS_PER_STEP)
            h, hn = _merge(a, o_b, gates, h, w_proj_a[l].astype(BF16), w_proj_b[l].astype(BF16),
                           w_out[l].astype(BF16), norm_ffn[l], _pick(seq, 256))
            h = _ffn(hn, h, w_gate_up[l].astype(BF16), w_down[l].astype(BF16),
                     norm_final, _pick(seq, 1024), _pick(w_down.shape[1], 512),
                     final_norm=(l == depth - 1))
        outs.append(h)
    return outs[0].reshape(x.shape) if batch == 1 else jnp.stack(outs)
```

```python
import functools
import math

import jax
import jax.numpy as jnp
from jax import lax
from jax.experimental import pallas as pl
from jax.experimental.pallas import tpu as pltpu

EPS = 1e-6
CHUNK = 128
SGU_GROUPS = 8
SGU_GROUP_DIM = 128
SB_HEADS = 8
SB_HEAD_DIM = 128
ATT_BLOCK = 256
STICK_USED_F32_ZERO = 104.0
ATT_HEADS_PER_STEP = 4
VMEM_LIMIT_BYTES = 60 * 1024 * 1024

F32 = jnp.float32
BF16 = jnp.bfloat16


def _params(*sem):
    return pltpu.CompilerParams(dimension_semantics=sem, vmem_limit_bytes=VMEM_LIMIT_BYTES)


def _exp(x):
    return jnp.exp2(x * math.log2(math.e))


def _gelu(x):
    return 0.5 * x * (1.0 + lax.erf(x * math.sqrt(0.5)))


def _sgu_kernel(x_ref, gn_ref, w_ref, lng_ref, lnb_ref, ws_ref, bs_ref, xn_ref, o_ref):
    width = o_ref.shape[1]
    x = x_ref[...]
    ms = jnp.mean(x * x, axis=-1, keepdims=True)
    xn = (x * lax.rsqrt(ms + EPS) * gn_ref[...]).astype(BF16)
    xn_ref[...] = xn
    v = _gelu(jnp.dot(xn, w_ref[:, width:], preferred_element_type=F32))
    mu = jnp.mean(v, axis=-1, keepdims=True)
    cen = v - mu
    var = jnp.mean(cen * cen, axis=-1, keepdims=True)
    vln = (cen * lax.rsqrt(var + EPS) * lng_ref[...] + lnb_ref[...]).astype(BF16)
    u = _gelu(jnp.dot(xn, w_ref[:, :width], preferred_element_type=F32))
    t_idx = lax.broadcasted_iota(jnp.int32, (CHUNK, CHUNK), 0)
    s_idx = lax.broadcasted_iota(jnp.int32, (CHUNK, CHUNK), 1)
    causal = s_idx <= t_idx
    for g in range(SGU_GROUPS):
        wm = jnp.where(causal, ws_ref[g], 0.0).astype(BF16)
        cols = slice(g * SGU_GROUP_DIM, (g + 1) * SGU_GROUP_DIM)
        for c in range(o_ref.shape[0] // CHUNK):
            rows = slice(c * CHUNK, (c + 1) * CHUNK)
            mixed = jnp.dot(wm, vln[rows, cols], preferred_element_type=F32) + bs_ref[g]
            o_ref[rows, cols] = (u[rows, cols] * mixed).astype(o_ref.dtype)


def _sgu(x, g_norm, w_in, ln_g, ln_b, w_s, b_s, tm):
    s, d = x.shape
    width = SGU_GROUPS * SGU_GROUP_DIM
    return pl.pallas_call(
        _sgu_kernel,
        grid=(s // tm,),
        in_specs=[pl.BlockSpec((tm, d), lambda i: (i, 0)),
                  pl.BlockSpec((1, d), lambda i: (0, 0)),
                  pl.BlockSpec((d, 2 * width), lambda i: (0, 0), pipeline_mode=pl.Buffered(1)),
                  pl.BlockSpec((1, width), lambda i: (0, 0)),
                  pl.BlockSpec((1, width), lambda i: (0, 0)),
                  pl.BlockSpec(w_s.shape, lambda i: (0, 0, 0)),
                  pl.BlockSpec((SGU_GROUPS, CHUNK, 1), lambda i: (0, 0, 0))],
        out_specs=[pl.BlockSpec((tm, d), lambda i: (i, 0)),
                   pl.BlockSpec((tm, width), lambda i: (i, 0))],
        out_shape=[jax.ShapeDtypeStruct((s, d), BF16),
                   jax.ShapeDtypeStruct((s, width), BF16)],
        compiler_params=_params("arbitrary"),
        name="sgu",
    )(x, g_norm.reshape(1, d), w_in, ln_g.reshape(1, width), ln_b.reshape(1, width), w_s,
      b_s.reshape(SGU_GROUPS, CHUNK, 1))


def _proj_kernel(xn_ref, *refs, gate):
    o_ref = refs[-1]
    w_refs = refs[:-2] if gate else refs[:-1]
    tn = w_refs[0].shape[1]
    xn = xn_ref[...]
    for k, w_ref in enumerate(w_refs):
        cols = slice(k * tn, (k + 1) * tn)
        acc = jnp.dot(xn, w_ref[...], preferred_element_type=F32)
        if gate:
            acc = jax.nn.sigmoid(acc + refs[-2][:, cols])
        o_ref[:, cols] = acc.astype(o_ref.dtype)


def _proj(xn, w_in, col0, n, tm, tn, out_dtype, bias=None):
    s, d = xn.shape
    j0 = col0 // tn
    in_specs = [pl.BlockSpec((tm, d), lambda i: (i, 0))]
    args = [xn]
    for k in range(n // tn):
        in_specs.append(pl.BlockSpec((d, tn), lambda i, k=k: (0, j0 + k),
                                     pipeline_mode=pl.Buffered(1)))
        args.append(w_in)
    if bias is not None:
        in_specs.append(pl.BlockSpec((1, n), lambda i: (0, 0)))
        args.append(bias.reshape(1, n))
    return pl.pallas_call(
        functools.partial(_proj_kernel, gate=bias is not None),
        grid=(s // tm,),
        in_specs=in_specs,
        out_specs=pl.BlockSpec((tm, n), lambda i: (i, 0)),
        out_shape=jax.ShapeDtypeStruct((s, n), out_dtype),
        compiler_params=_params("arbitrary"),
        name="gates" if bias is not None else "qkv",
    )(*args)


def _attn_kernel(q_ref, k_ref, v_ref, o_ref, acc_ref, used_ref, *, heads):
    t = q_ref.shape[0]
    dh = SB_HEAD_DIM
    qi = pl.program_id(1)
    scale = 1.0 / math.sqrt(dh)
    row = lax.broadcasted_iota(jnp.int32, (t, t), 0)
    col = lax.broadcasted_iota(jnp.int32, (t, t), 1)
    later = (row > col).astype(BF16)
    causal = col < row
    has_prev = qi > 0
    kb_prev = jnp.maximum(qi - 1, 0)

    def block(h, kb, mask):
        cols = slice(h * dh, (h + 1) * dh)
        kk = k_ref[pl.ds(kb * t, t), cols]
        z = lax.dot_general(q_ref[:, cols], kk, (((1,), (1,)), ((), ())),
                            preferred_element_type=F32) * scale
        neg_abs = lax.bitcast_convert_type(
            lax.bitcast_convert_type(z, jnp.uint32) | jnp.uint32(0x80000000), F32)
        used = jnp.maximum(z, 0.0) + jnp.log(1.0 + _exp(neg_abs))
        log_beta = z - used
        if mask is not None:
            used = jnp.where(mask, used, 0.0)
        after = jnp.dot(used.astype(BF16), later, preferred_element_type=F32)
        return log_beta - after, jnp.sum(used, axis=-1, keepdims=True)

    def pv(h, kb, w):
        cols = slice(h * dh, (h + 1) * dh)
        return jnp.dot(w.astype(BF16), v_ref[pl.ds(kb * t, t), cols], preferred_element_type=F32)

    low = None
    for h in range(heads):
        cols = slice(h * dh, (h + 1) * dh)
        logw_d, used_d = block(h, qi, causal)
        acc = pv(h, qi, jnp.where(causal, _exp(logw_d), 0.0))
        logw_p, used_p = block(h, kb_prev, None)
        acc_p = pv(h, kb_prev, _exp(logw_p - used_d))
        acc_ref[:, cols] = acc + jnp.where(has_prev, acc_p, 0.0)
        used = used_d + used_p
        used_ref[h] = used
        low = jnp.min(used) if low is None else jnp.minimum(low, jnp.min(used))

    def cond(state):
        kb, low = state
        return jnp.logical_and(kb >= 0, low <= STICK_USED_F32_ZERO)

    def body(state):
        kb, _ = state
        low = None
        for h in range(heads):
            cols = slice(h * dh, (h + 1) * dh)
            logw, used_b = block(h, kb, None)
            used = used_ref[h]
            acc_ref[:, cols] += pv(h, kb, _exp(logw - used))
            used = used + used_b
            used_ref[h] = used
            low = jnp.min(used) if low is None else jnp.minimum(low, jnp.min(used))
        return kb - 1, low

    lax.while_loop(cond, body, (qi - 2, low))
    o_ref[...] = acc_ref[...].astype(o_ref.dtype)


def _attention(qkv, heads):
    s = qkv.shape[0]
    t = ATT_BLOCK
    w = heads * SB_HEAD_DIM
    groups = SB_HEADS // heads
    resident = dict(pipeline_mode=pl.Buffered(1))
    return pl.pallas_call(
        functools.partial(_attn_kernel, heads=heads),
        grid=(groups, s // t),
        in_specs=[pl.BlockSpec((t, w), lambda g, i: (i, g)),
                  pl.BlockSpec((s, w), lambda g, i: (0, groups + g), **resident),
                  pl.BlockSpec((s, w), lambda g, i: (0, 2 * groups + g), **resident)],
        out_specs=pl.BlockSpec((t, w), lambda g, i: (i, g)),
        out_shape=jax.ShapeDtypeStruct((s, SB_HEADS * SB_HEAD_DIM), BF16),
        scratch_shapes=[pltpu.VMEM((t, w), F32), pltpu.VMEM((heads, t, 1), F32)],
        compiler_params=_params("arbitrary", "arbitrary"),
        name="sb_attention",
    )(qkv, qkv, qkv)


def _merge_kernel(a_ref, b_ref, g_ref, x_ref, wa_ref, wb_ref, wo_ref, gn_ref, o_ref, hn_ref):
    d = x_ref.shape[1]
    y_a = jnp.dot(a_ref[...], wa_ref[...], preferred_element_type=F32)
    y_b = jnp.dot(b_ref[...], wb_ref[...], preferred_element_type=F32)
    m = g_ref[:, :d] * y_a + g_ref[:, d:] * y_b
    y = x_ref[...] + jnp.dot(m.astype(BF16), wo_ref[...], preferred_element_type=F32)
    o_ref[...] = y
    ms = jnp.mean(y * y, axis=-1, keepdims=True)
    hn_ref[...] = (y * lax.rsqrt(ms + EPS) * gn_ref[...]).astype(BF16)


def _merge(a, b, gates, x, w_a, w_b, w_o, g_norm, tm):
    s, d = x.shape
    wa = a.shape[1]
    wb = b.shape[1]
    resident = dict(pipeline_mode=pl.Buffered(1))
    return pl.pallas_call(
        _merge_kernel,
        grid=(s // tm,),
        in_specs=[pl.BlockSpec((tm, wa), lambda i: (i, 0)),
                  pl.BlockSpec((tm, wb), lambda i: (i, 0)),
                  pl.BlockSpec((tm, 2 * d), lambda i: (i, 0)),
                  pl.BlockSpec((tm, d), lambda i: (i, 0)),
                  pl.BlockSpec((wa, d), lambda i: (0, 0), **resident),
                  pl.BlockSpec((wb, d), lambda i: (0, 0), **resident),
                  pl.BlockSpec((d, d), lambda i: (0, 0), **resident),
                  pl.BlockSpec((1, d), lambda i: (0, 0))],
        out_specs=[pl.BlockSpec((tm, d), lambda i: (i, 0)),
                   pl.BlockSpec((tm, d), lambda i: (i, 0))],
        out_shape=[jax.ShapeDtypeStruct((s, d), F32),
                   jax.ShapeDtypeStruct((s, d), BF16)],
        compiler_params=_params("arbitrary"),
        name="merge",
    )(a, b, gates, x, w_a, w_b, w_o, g_norm.reshape(1, d))


def _ffn_kernel(hn_ref, x_hbm, wg_ref, wu_ref, wd_ref, gf_ref, o_ref, x_scr, x_sem, *, final_norm):
    i = pl.program_id(0)
    f = pl.program_id(1)
    tm = o_ref.shape[0]
    x_copy = pltpu.make_async_copy(x_hbm.at[pl.ds(pl.multiple_of(i * tm, tm), tm), :], x_scr, x_sem)

    @pl.when(f == 0)
    def _():
        x_copy.start()
        o_ref[...] = jnp.zeros_like(o_ref)

    hn = hn_ref[...]
    gate = jnp.dot(hn, wg_ref[...], preferred_element_type=F32)
    up = jnp.dot(hn, wu_ref[...], preferred_element_type=F32)
    h = (gate * jax.nn.sigmoid(gate) * up).astype(BF16)
    tn = _pick(o_ref.shape[1], wg_ref.shape[1])
    for c in range(o_ref.shape[1] // tn):
        cols = slice(c * tn, (c + 1) * tn)
        o_ref[:, cols] += jnp.dot(h, wd_ref[:, cols], preferred_element_type=F32)

    @pl.when(f == pl.num_programs(1) - 1)
    def _():
        x_copy.wait()
        y = x_scr[...] + o_ref[...]
        if final_norm:
            ms = jnp.mean(y * y, axis=-1, keepdims=True)
            y = y * lax.rsqrt(ms + EPS) * gf_ref[...]
        o_ref[...] = y


def _ffn(hn, x, w_gate_up, w_down, g_final, tm, tf, final_norm):
    s, d = x.shape
    d_ff = w_down.shape[0]
    nf = d_ff // tf
    return pl.pallas_call(
        functools.partial(_ffn_kernel, final_norm=final_norm),
        grid=(s // tm, nf),
        in_specs=[pl.BlockSpec((tm, d), lambda i, f: (i, 0)),
                  pl.BlockSpec(memory_space=pl.ANY),
                  pl.BlockSpec((d, tf), lambda i, f: (0, f)),
                  pl.BlockSpec((d, tf), lambda i, f: (0, nf + f)),
                  pl.BlockSpec((tf, d), lambda i, f: (f, 0)),
                  pl.BlockSpec((1, d), lambda i, f: (0, 0))],
        out_specs=pl.BlockSpec((tm, d), lambda i, f: (i, 0)),
        out_shape=jax.ShapeDtypeStruct((s, d), F32),
        scratch_shapes=[pltpu.VMEM((tm, d), F32), pltpu.SemaphoreType.DMA(())],
        compiler_params=_params("arbitrary", "arbitrary"),
        name="ffn",
    )(hn, x, w_gate_up, w_gate_up, w_down, g_final.reshape(1, d))


def _pick(n, pref):
    return pref if n % pref == 0 else n


def kernel(x, norm_mix, w_in, b_gate, ln_v_g, ln_v_b, w_s, b_s, w_proj_a, w_proj_b,
           w_out, norm_ffn, w_gate_up, w_down, norm_final):
    batch, seq, d = x.shape
    depth = w_in.shape[0]
    sgu_w = SGU_GROUPS * SGU_GROUP_DIM
    sb_w = SB_HEADS * SB_HEAD_DIM
    off_q = 2 * sgu_w
    off_g = off_q + 3 * sb_w
    outs = []
    for b in range(batch):
        h = x.reshape(seq, d) if batch == 1 else x[b]
        for l in range(depth):
            w_in_l = w_in[l].astype(BF16)
            xn, a = _sgu(h, norm_mix[l], w_in_l, ln_v_g[l], ln_v_b[l], w_s[l], b_s[l],
                         _pick(seq, 512))
            qkv = _proj(xn, w_in_l, off_q, 3 * sb_w, _pick(seq, 1024), sb_w, BF16)
            gates = _proj(xn, w_in_l, off_g, 2 * d, _pick(seq, 512), _pick(2 * d, 1024), F32,
                          bias=b_gate[l])
            o_b = _attention(qkv, ATT_HEADS_PER_STEP)
            h, hn = _merge(a, o_b, gates, h, w_proj_a[l].astype(BF16), w_proj_b[l].astype(BF16),
                           w_out[l].astype(BF16), norm_ffn[l], _pick(seq, 256))
            h = _ffn(hn, h, w_gate_up[l].astype(BF16), w_down[l].astype(BF16),
                     norm_final, _pick(seq, 1024), _pick(w_down.shape[1], 512),
                     final_norm=(l == depth - 1))
        outs.append(h)
    return outs[0].reshape(x.shape) if batch == 1 else jnp.stack(outs)
```

```python
import functools
import math

import jax
import jax.numpy as jnp
from jax import lax
from jax.experimental import pallas as pl
from jax.experimental.pallas import tpu as pltpu

EPS = 1e-6
CHUNK = 128
SGU_GROUPS = 8
SGU_GROUP_DIM = 128
SB_HEADS = 8
SB_HEAD_DIM = 128
ATT_BLOCK = 256
STICK_USED_F32_ZERO = 104.0
VMEM_LIMIT_BYTES = 60 * 1024 * 1024

F32 = jnp.float32
BF16 = jnp.bfloat16


def _params(*sem):
    return pltpu.CompilerParams(dimension_semantics=sem, vmem_limit_bytes=VMEM_LIMIT_BYTES)


def _exp(x):
    return jnp.exp2(x * math.log2(math.e))


def _gelu(x):
    return 0.5 * x * (1.0 + lax.erf(x * math.sqrt(0.5)))


def _sgu_kernel(x_ref, gn_ref, w_ref, wq_ref, wk_ref, wv_ref, lng_ref, lnb_ref, ws_ref, bs_ref,
                xn_ref, o_ref, qkv_ref):
    width = o_ref.shape[1]
    x = x_ref[...]
    ms = jnp.mean(x * x, axis=-1, keepdims=True)
    xn = (x * lax.rsqrt(ms + EPS) * gn_ref[...]).astype(BF16)
    xn_ref[...] = xn
    v = _gelu(jnp.dot(xn, w_ref[:, width:], preferred_element_type=F32))
    qkv_ref[:, :width] = jnp.dot(xn, wq_ref[...], preferred_element_type=F32).astype(BF16)
    mu = jnp.mean(v, axis=-1, keepdims=True)
    cen = v - mu
    var = jnp.mean(cen * cen, axis=-1, keepdims=True)
    vln = (cen * lax.rsqrt(var + EPS) * lng_ref[...] + lnb_ref[...]).astype(BF16)
    u = _gelu(jnp.dot(xn, w_ref[:, :width], preferred_element_type=F32))
    qkv_ref[:, width:2 * width] = jnp.dot(xn, wk_ref[...], preferred_element_type=F32).astype(BF16)
    qkv_ref[:, 2 * width:] = jnp.dot(xn, wv_ref[...], preferred_element_type=F32).astype(BF16)
    t_idx = lax.broadcasted_iota(jnp.int32, (CHUNK, CHUNK), 0)
    s_idx = lax.broadcasted_iota(jnp.int32, (CHUNK, CHUNK), 1)
    causal = s_idx <= t_idx
    for g in range(SGU_GROUPS):
        wm = jnp.where(causal, ws_ref[g], 0.0).astype(BF16)
        cols = slice(g * SGU_GROUP_DIM, (g + 1) * SGU_GROUP_DIM)
        for c in range(o_ref.shape[0] // CHUNK):
            rows = slice(c * CHUNK, (c + 1) * CHUNK)
            mixed = jnp.dot(wm, vln[rows, cols], preferred_element_type=F32) + bs_ref[g]
            o_ref[rows, cols] = (u[rows, cols] * mixed).astype(o_ref.dtype)


def _sgu(x, g_norm, w_in, ln_g, ln_b, w_s, b_s, tm):
    s, d = x.shape
    width = SGU_GROUPS * SGU_GROUP_DIM
    resident = dict(pipeline_mode=pl.Buffered(1))
    return pl.pallas_call(
        _sgu_kernel,
        grid=(s // tm,),
        in_specs=[pl.BlockSpec((tm, d), lambda i: (i, 0)),
                  pl.BlockSpec((1, d), lambda i: (0, 0)),
                  pl.BlockSpec((d, 2 * width), lambda i: (0, 0), **resident),
                  pl.BlockSpec((d, width), lambda i: (0, 2), **resident),
                  pl.BlockSpec((d, width), lambda i: (0, 3), **resident),
                  pl.BlockSpec((d, width), lambda i: (0, 4), **resident),
                  pl.BlockSpec((1, width), lambda i: (0, 0)),
                  pl.BlockSpec((1, width), lambda i: (0, 0)),
                  pl.BlockSpec(w_s.shape, lambda i: (0, 0, 0)),
                  pl.BlockSpec((SGU_GROUPS, CHUNK, 1), lambda i: (0, 0, 0))],
        out_specs=[pl.BlockSpec((tm, d), lambda i: (i, 0)),
                   pl.BlockSpec((tm, width), lambda i: (i, 0)),
                   pl.BlockSpec((tm, 3 * width), lambda i: (i, 0))],
        out_shape=[jax.ShapeDtypeStruct((s, d), BF16),
                   jax.ShapeDtypeStruct((s, width), BF16),
                   jax.ShapeDtypeStruct((s, 3 * width), BF16)],
        compiler_params=_params("arbitrary"),
        name="sgu_qkv",
    )(x, g_norm.reshape(1, d), w_in, w_in, w_in, w_in, ln_g.reshape(1, width),
      ln_b.reshape(1, width), w_s, b_s.reshape(SGU_GROUPS, CHUNK, 1))


def _proj_kernel(xn_ref, *refs, gate):
    o_ref = refs[-1]
    w_refs = refs[:-2] if gate else refs[:-1]
    tn = w_refs[0].shape[1]
    xn = xn_ref[...]
    for k, w_ref in enumerate(w_refs):
        cols = slice(k * tn, (k + 1) * tn)
        acc = jnp.dot(xn, w_ref[...], preferred_element_type=F32)
        if gate:
            acc = jax.nn.sigmoid(acc + refs[-2][:, cols])
        o_ref[:, cols] = acc.astype(o_ref.dtype)


def _proj(xn, w_in, col0, n, tm, tn, out_dtype, bias=None):
    s, d = xn.shape
    j0 = col0 // tn
    in_specs = [pl.BlockSpec((tm, d), lambda i: (i, 0))]
    args = [xn]
    for k in range(n // tn):
        in_specs.append(pl.BlockSpec((d, tn), lambda i, k=k: (0, j0 + k),
                                     pipeline_mode=pl.Buffered(1)))
        args.append(w_in)
    if bias is not None:
        in_specs.append(pl.BlockSpec((1, n), lambda i: (0, 0)))
        args.append(bias.reshape(1, n))
    return pl.pallas_call(
        functools.partial(_proj_kernel, gate=bias is not None),
        grid=(s // tm,),
        in_specs=in_specs,
        out_specs=pl.BlockSpec((tm, n), lambda i: (i, 0)),
        out_shape=jax.ShapeDtypeStruct((s, n), out_dtype),
        compiler_params=_params("arbitrary"),
        name="gates" if bias is not None else "qkv",
    )(*args)


def _attn_kernel(q_ref, kd_ref, kp_ref, vd_ref, vp_ref, qkv_hbm, o_ref,
                 acc_ref, used_ref, kbuf, vbuf, sems):
    t = q_ref.shape[0]
    dh = SB_HEAD_DIM
    width = q_ref.shape[1]
    heads = width // dh
    qi = pl.program_id(0)
    scale = 1.0 / math.sqrt(dh)
    row = lax.broadcasted_iota(jnp.int32, (t, t), 0)
    col = lax.broadcasted_iota(jnp.int32, (t, t), 1)
    later = (row > col).astype(BF16)
    causal = col < row
    has_prev = qi > 0

    def block(h, k_ref, mask):
        cols = slice(h * dh, (h + 1) * dh)
        z = lax.dot_general(q_ref[:, cols], k_ref[:, cols], (((1,), (1,)), ((), ())),
                            preferred_element_type=F32) * scale
        used = jnp.maximum(z, 0.0) + jnp.log(1.0 + _exp(-jnp.abs(z)))
        log_beta = z - used
        if mask is not None:
            used = jnp.where(mask, used, 0.0)
        after = jnp.dot(used.astype(BF16), later, preferred_element_type=F32)
        return log_beta - after, jnp.sum(used, axis=-1, keepdims=True)

    def pv(h, v_ref, w):
        cols = slice(h * dh, (h + 1) * dh)
        return jnp.dot(w.astype(BF16), v_ref[:, cols], preferred_element_type=F32)

    low = None
    for h in range(heads):
        cols = slice(h * dh, (h + 1) * dh)
        logw_d, used_d = block(h, kd_ref, causal)
        acc = pv(h, vd_ref, jnp.where(causal, _exp(logw_d), 0.0))
        logw_p, used_p = block(h, kp_ref, None)
        acc_p = pv(h, vp_ref, _exp(logw_p - used_d))
        acc_ref[:, cols] = acc + jnp.where(has_prev, acc_p, 0.0)
        used = used_d + used_p
        used_ref[h] = used
        low = jnp.min(used) if low is None else jnp.minimum(low, jnp.min(used))

    def cond(state):
        kb, low = state
        return jnp.logical_and(kb >= 0, low <= STICK_USED_F32_ZERO)

    def body(state):
        kb, _ = state
        rows = pl.ds(pl.multiple_of(kb * t, t), t)
        k_copy = pltpu.make_async_copy(qkv_hbm.at[rows, pl.ds(width, width)], kbuf, sems.at[0])
        v_copy = pltpu.make_async_copy(qkv_hbm.at[rows, pl.ds(2 * width, width)], vbuf, sems.at[1])
        k_copy.start()
        v_copy.start()
        k_copy.wait()
        v_copy.wait()
        low = None
        for h in range(heads):
            cols = slice(h * dh, (h + 1) * dh)
            logw, used_b = block(h, kbuf, None)
            used = used_ref[h]
            acc_ref[:, cols] += pv(h, vbuf, _exp(logw - used))
            used = used + used_b
            used_ref[h] = used
            low = jnp.min(used) if low is None else jnp.minimum(low, jnp.min(used))
        return kb - 1, low

    lax.while_loop(cond, body, (qi - 2, low))
    o_ref[...] = acc_ref[...].astype(o_ref.dtype)


def _attention(qkv):
    s = qkv.shape[0]
    t = ATT_BLOCK
    w = SB_HEADS * SB_HEAD_DIM
    prev = lambda i: jnp.maximum(i - 1, 0)
    return pl.pallas_call(
        _attn_kernel,
        grid=(s // t,),
        in_specs=[pl.BlockSpec((t, w), lambda i: (i, 0)),
                  pl.BlockSpec((t, w), lambda i: (i, 1)),
                  pl.BlockSpec((t, w), lambda i: (prev(i), 1)),
                  pl.BlockSpec((t, w), lambda i: (i, 2)),
                  pl.BlockSpec((t, w), lambda i: (prev(i), 2)),
                  pl.BlockSpec(memory_space=pl.ANY)],
        out_specs=pl.BlockSpec((t, w), lambda i: (i, 0)),
        out_shape=jax.ShapeDtypeStruct((s, w), BF16),
        scratch_shapes=[pltpu.VMEM((t, w), F32), pltpu.VMEM((SB_HEADS, t, 1), F32),
                        pltpu.VMEM((t, w), BF16), pltpu.VMEM((t, w), BF16),
                        pltpu.SemaphoreType.DMA((2,))],
        compiler_params=_params("arbitrary"),
        name="sb_attention",
    )(qkv, qkv, qkv, qkv, qkv, qkv)


def _merge_kernel(a_ref, b_ref, g_ref, x_ref, wa_ref, wb_ref, wo_ref, gn_ref, o_ref, hn_ref):
    d = x_ref.shape[1]
    y_a = jnp.dot(a_ref[...], wa_ref[...], preferred_element_type=F32)
    y_b = jnp.dot(b_ref[...], wb_ref[...], preferred_element_type=F32)
    m = g_ref[:, :d] * y_a + g_ref[:, d:] * y_b
    y = x_ref[...] + jnp.dot(m.astype(BF16), wo_ref[...], preferred_element_type=F32)
    o_ref[...] = y
    ms = jnp.mean(y * y, axis=-1, keepdims=True)
    hn_ref[...] = (y * lax.rsqrt(ms + EPS) * gn_ref[...]).astype(BF16)


def _merge(a, b, gates, x, w_a, w_b, w_o, g_norm, tm):
    s, d = x.shape
    wa = a.shape[1]
    wb = b.shape[1]
    resident = dict(pipeline_mode=pl.Buffered(1))
    return pl.pallas_call(
        _merge_kernel,
        grid=(s // tm,),
        in_specs=[pl.BlockSpec((tm, wa), lambda i: (i, 0)),
                  pl.BlockSpec((tm, wb), lambda i: (i, 0)),
                  pl.BlockSpec((tm, 2 * d), lambda i: (i, 0)),
                  pl.BlockSpec((tm, d), lambda i: (i, 0)),
                  pl.BlockSpec((wa, d), lambda i: (0, 0), **resident),
                  pl.BlockSpec((wb, d), lambda i: (0, 0), **resident),
                  pl.BlockSpec((d, d), lambda i: (0, 0), **resident),
                  pl.BlockSpec((1, d), lambda i: (0, 0))],
        out_specs=[pl.BlockSpec((tm, d), lambda i: (i, 0)),
                   pl.BlockSpec((tm, d), lambda i: (i, 0))],
        out_shape=[jax.ShapeDtypeStruct((s, d), F32),
                   jax.ShapeDtypeStruct((s, d), BF16)],
        compiler_params=_params("arbitrary"),
        name="merge",
    )(a, b, gates, x, w_a, w_b, w_o, g_norm.reshape(1, d))


def _ffn_kernel(hn_ref, x_hbm, wg_ref, wu_ref, wd_ref, gf_ref, o_ref, x_scr, x_sem, *, final_norm):
    i = pl.program_id(0)
    f = pl.program_id(1)
    tm = o_ref.shape[0]
    x_copy = pltpu.make_async_copy(x_hbm.at[pl.ds(pl.multiple_of(i * tm, tm), tm), :], x_scr, x_sem)

    @pl.when(f == 0)
    def _():
        x_copy.start()
        o_ref[...] = jnp.zeros_like(o_ref)

    hn = hn_ref[...]
    gate = jnp.dot(hn, wg_ref[...], preferred_element_type=F32)
    up = jnp.dot(hn, wu_ref[...], preferred_element_type=F32)
    h = (gate * jax.nn.sigmoid(gate) * up).astype(BF16)
    tn = _pick(o_ref.shape[1], wg_ref.shape[1])
    for c in range(o_ref.shape[1] // tn):
        cols = slice(c * tn, (c + 1) * tn)
        o_ref[:, cols] += jnp.dot(h, wd_ref[:, cols], preferred_element_type=F32)

    @pl.when(f == pl.num_programs(1) - 1)
    def _():
        x_copy.wait()
        y = x_scr[...] + o_ref[...]
        if final_norm:
            ms = jnp.mean(y * y, axis=-1, keepdims=True)
            y = y * lax.rsqrt(ms + EPS) * gf_ref[...]
        o_ref[...] = y


def _ffn(hn, x, w_gate_up, w_down, g_final, tm, tf, final_norm):
    s, d = x.shape
    d_ff = w_down.shape[0]
    nf = d_ff // tf
    return pl.pallas_call(
        functools.partial(_ffn_kernel, final_norm=final_norm),
        grid=(s // tm, nf),
        in_specs=[pl.BlockSpec((tm, d), lambda i, f: (i, 0)),
                  pl.BlockSpec(memory_space=pl.ANY),
                  pl.BlockSpec((d, tf), lambda i, f: (0, f)),
                  pl.BlockSpec((d, tf), lambda i, f: (0, nf + f)),
                  pl.BlockSpec((tf, d), lambda i, f: (f, 0)),
                  pl.BlockSpec((1, d), lambda i, f: (0, 0))],
        out_specs=pl.BlockSpec((tm, d), lambda i, f: (i, 0)),
        out_shape=jax.ShapeDtypeStruct((s, d), F32),
        scratch_shapes=[pltpu.VMEM((tm, d), F32), pltpu.SemaphoreType.DMA(())],
        compiler_params=_params("arbitrary", "arbitrary"),
        name="ffn",
    )(hn, x, w_gate_up, w_gate_up, w_down, g_final.reshape(1, d))


def _pick(n, pref):
    return pref if n % pref == 0 else n


def kernel(x, norm_mix, w_in, b_gate, ln_v_g, ln_v_b, w_s, b_s, w_proj_a, w_proj_b,
           w_out, norm_ffn, w_gate_up, w_down, norm_final):
    batch, seq, d = x.shape
    depth = w_in.shape[0]
    sgu_w = SGU_GROUPS * SGU_GROUP_DIM
    sb_w = SB_HEADS * SB_HEAD_DIM
    assert sgu_w == sb_w
    off_q = 2 * sgu_w
    off_g = off_q + 3 * sb_w
    outs = []
    for b in range(batch):
        h = x.reshape(seq, d) if batch == 1 else x[b]
        for l in range(depth):
            w_in_l = w_in[l].astype(BF16)
            xn, a, qkv = _sgu(h, norm_mix[l], w_in_l, ln_v_g[l], ln_v_b[l], w_s[l], b_s[l],
                              _pick(seq, 512))
            gates = _proj(xn, w_in_l, off_g, 2 * d, _pick(seq, 512), _pick(2 * d, 1024), F32,
                          bias=b_gate[l])
            o_b = _attention(qkv)
            h, hn = _merge(a, o_b, gates, h, w_proj_a[l].astype(BF16), w_proj_b[l].astype(BF16),
                           w_out[l].astype(BF16), norm_ffn[l], _pick(seq, 256))
            h = _ffn(hn, h, w_gate_up[l].astype(BF16), w_down[l].astype(BF16),
                     norm_final, _pick(seq, 1024), _pick(w_down.shape[1], 512),
                     final_norm=(l == depth - 1))
        outs.append(h)
    return outs[0].reshape(x.shape) if batch == 1 else jnp.stack(outs)
```

```python
import functools
import math

import jax
import jax.numpy as jnp
from jax import lax
from jax.experimental import pallas as pl
from jax.experimental.pallas import tpu as pltpu

EPS = 1e-6
CHUNK = 128
SGU_GROUPS = 8
SGU_GROUP_DIM = 128
SB_HEADS = 8
SB_HEAD_DIM = 128
ATT_BLOCK = 256
STICK_USED_F32_ZERO = 104.0
VMEM_LIMIT_BYTES = 60 * 1024 * 1024

F32 = jnp.float32
BF16 = jnp.bfloat16


def _params(*sem):
    return pltpu.CompilerParams(dimension_semantics=sem, vmem_limit_bytes=VMEM_LIMIT_BYTES)


def _exp(x):
    return jnp.exp2(x * math.log2(math.e))


def _gelu(x):
    return 0.5 * x * (1.0 + lax.erf(x * math.sqrt(0.5)))


def _sgu_kernel(x_ref, gn_ref, w_ref, wq_ref, wk_ref, wv_ref, lng_ref, lnb_ref, ws_ref, bs_ref,
                xn_ref, o_ref, qkv_ref):
    width = o_ref.shape[1]
    x = x_ref[...]
    ms = jnp.mean(x * x, axis=-1, keepdims=True)
    xn = (x * lax.rsqrt(ms + EPS) * gn_ref[...]).astype(BF16)
    xn_ref[...] = xn
    v = _gelu(jnp.dot(xn, w_ref[:, width:], preferred_element_type=F32))
    qkv_ref[:, :width] = jnp.dot(xn, wq_ref[...], preferred_element_type=F32).astype(BF16)
    mu = jnp.mean(v, axis=-1, keepdims=True)
    cen = v - mu
    var = jnp.mean(cen * cen, axis=-1, keepdims=True)
    vln = (cen * lax.rsqrt(var + EPS) * lng_ref[...] + lnb_ref[...]).astype(BF16)
    u = _gelu(jnp.dot(xn, w_ref[:, :width], preferred_element_type=F32))
    qkv_ref[:, width:2 * width] = jnp.dot(xn, wk_ref[...], preferred_element_type=F32).astype(BF16)
    qkv_ref[:, 2 * width:] = jnp.dot(xn, wv_ref[...], preferred_element_type=F32).astype(BF16)
    t_idx = lax.broadcasted_iota(jnp.int32, (CHUNK, CHUNK), 0)
    s_idx = lax.broadcasted_iota(jnp.int32, (CHUNK, CHUNK), 1)
    causal = s_idx <= t_idx
    for g in range(SGU_GROUPS):
        wm = jnp.where(causal, ws_ref[g], 0.0).astype(BF16)
        cols = slice(g * SGU_GROUP_DIM, (g + 1) * SGU_GROUP_DIM)
        for c in range(o_ref.shape[0] // CHUNK):
            rows = slice(c * CHUNK, (c + 1) * CHUNK)
            mixed = jnp.dot(wm, vln[rows, cols], preferred_element_type=F32) + bs_ref[g]
            o_ref[rows, cols] = (u[rows, cols] * mixed).astype(o_ref.dtype)


def _sgu(x, g_norm, w_in, ln_g, ln_b, w_s, b_s, tm):
    s, d = x.shape
    width = SGU_GROUPS * SGU_GROUP_DIM
    resident = dict(pipeline_mode=pl.Buffered(1))
    return pl.pallas_call(
        _sgu_kernel,
        grid=(s // tm,),
        in_specs=[pl.BlockSpec((tm, d), lambda i: (i, 0)),
                  pl.BlockSpec((1, d), lambda i: (0, 0)),
                  pl.BlockSpec((d, 2 * width), lambda i: (0, 0), **resident),
                  pl.BlockSpec((d, width), lambda i: (0, 2), **resident),
                  pl.BlockSpec((d, width), lambda i: (0, 3), **resident),
                  pl.BlockSpec((d, width), lambda i: (0, 4), **resident),
                  pl.BlockSpec((1, width), lambda i: (0, 0)),
                  pl.BlockSpec((1, width), lambda i: (0, 0)),
                  pl.BlockSpec(w_s.shape, lambda i: (0, 0, 0)),
                  pl.BlockSpec((SGU_GROUPS, CHUNK, 1), lambda i: (0, 0, 0))],
        out_specs=[pl.BlockSpec((tm, d), lambda i: (i, 0)),
                   pl.BlockSpec((tm, width), lambda i: (i, 0)),
                   pl.BlockSpec((tm, 3 * width), lambda i: (i, 0))],
        out_shape=[jax.ShapeDtypeStruct((s, d), BF16),
                   jax.ShapeDtypeStruct((s, width), BF16),
                   jax.ShapeDtypeStruct((s, 3 * width), BF16)],
        compiler_params=_params("arbitrary"),
        name="sgu_qkv",
    )(x, g_norm.reshape(1, d), w_in, w_in, w_in, w_in, ln_g.reshape(1, width),
      ln_b.reshape(1, width), w_s, b_s.reshape(SGU_GROUPS, CHUNK, 1))


def _attn_kernel(q_ref, kd_ref, kp_ref, vd_ref, vp_ref, qkv_hbm, o_ref,
                 acc_ref, used_ref, kbuf, vbuf, sems):
    t = q_ref.shape[0]
    dh = SB_HEAD_DIM
    width = q_ref.shape[1]
    heads = width // dh
    qi = pl.program_id(0)
    scale = 1.0 / math.sqrt(dh)
    row = lax.broadcasted_iota(jnp.int32, (t, t), 0)
    col = lax.broadcasted_iota(jnp.int32, (t, t), 1)
    later = (row > col).astype(BF16)
    causal = col < row
    has_prev = qi > 0

    def block(h, k_ref, mask):
        cols = slice(h * dh, (h + 1) * dh)
        z = lax.dot_general(q_ref[:, cols], k_ref[:, cols], (((1,), (1,)), ((), ())),
                            preferred_element_type=F32) * scale
        used = jnp.maximum(z, 0.0) + jnp.log(1.0 + _exp(-jnp.abs(z)))
        log_beta = z - used
        if mask is not None:
            used = jnp.where(mask, used, 0.0)
        after = jnp.dot(used.astype(BF16), later, preferred_element_type=F32)
        return log_beta - after, jnp.sum(used, axis=-1, keepdims=True)

    def pv(h, v_ref, w):
        cols = slice(h * dh, (h + 1) * dh)
        return jnp.dot(w.astype(BF16), v_ref[:, cols], preferred_element_type=F32)

    low = None
    for h in range(heads):
        cols = slice(h * dh, (h + 1) * dh)
        logw_d, used_d = block(h, kd_ref, causal)
        acc = pv(h, vd_ref, jnp.where(causal, _exp(logw_d), 0.0))
        logw_p, used_p = block(h, kp_ref, None)
        acc_p = pv(h, vp_ref, _exp(logw_p - used_d))
        acc_ref[:, cols] = acc + jnp.where(has_prev, acc_p, 0.0)
        used = used_d + used_p
        used_ref[h] = used
        low = jnp.min(used) if low is None else jnp.minimum(low, jnp.min(used))

    def cond(state):
        kb, low = state
        return jnp.logical_and(kb >= 0, low <= STICK_USED_F32_ZERO)

    def body(state):
        kb, _ = state
        rows = pl.ds(pl.multiple_of(kb * t, t), t)
        k_copy = pltpu.make_async_copy(qkv_hbm.at[rows, pl.ds(width, width)], kbuf, sems.at[0])
        v_copy = pltpu.make_async_copy(qkv_hbm.at[rows, pl.ds(2 * width, width)], vbuf, sems.at[1])
        k_copy.start()
        v_copy.start()
        k_copy.wait()
        v_copy.wait()
        low = None
        for h in range(heads):
            cols = slice(h * dh, (h + 1) * dh)
            logw, used_b = block(h, kbuf, None)
            used = used_ref[h]
            acc_ref[:, cols] += pv(h, vbuf, _exp(logw - used))
            used = used + used_b
            used_ref[h] = used
            low = jnp.min(used) if low is None else jnp.minimum(low, jnp.min(used))
        return kb - 1, low

    lax.while_loop(cond, body, (qi - 2, low))
    o_ref[...] = acc_ref[...].astype(o_ref.dtype)


def _attention(qkv):
    s = qkv.shape[0]
    t = ATT_BLOCK
    w = SB_HEADS * SB_HEAD_DIM
    prev = lambda i: jnp.maximum(i - 1, 0)
    return pl.pallas_call(
        _attn_kernel,
        grid=(s // t,),
        in_specs=[pl.BlockSpec((t, w), lambda i: (i, 0)),
                  pl.BlockSpec((t, w), lambda i: (i, 1)),
                  pl.BlockSpec((t, w), lambda i: (prev(i), 1)),
                  pl.BlockSpec((t, w), lambda i: (i, 2)),
                  pl.BlockSpec((t, w), lambda i: (prev(i), 2)),
                  pl.BlockSpec(memory_space=pl.ANY)],
        out_specs=pl.BlockSpec((t, w), lambda i: (i, 0)),
        out_shape=jax.ShapeDtypeStruct((s, w), BF16),
        scratch_shapes=[pltpu.VMEM((t, w), F32), pltpu.VMEM((SB_HEADS, t, 1), F32),
                        pltpu.VMEM((t, w), BF16), pltpu.VMEM((t, w), BF16),
                        pltpu.SemaphoreType.DMA((2,))],
        compiler_params=_params("arbitrary"),
        name="sb_attention",
    )(qkv, qkv, qkv, qkv, qkv, qkv)


def _ffn_kernel(hn_ref, x_hbm, wg_ref, wu_ref, wd_ref, gf_ref, o_ref, x_scr, x_sem, *, final_norm):
    i = pl.program_id(0)
    f = pl.program_id(1)
    tm = o_ref.shape[0]
    x_copy = pltpu.make_async_copy(x_hbm.at[pl.ds(pl.multiple_of(i * tm, tm), tm), :], x_scr, x_sem)

    @pl.when(f == 0)
    def _():
        x_copy.start()
        o_ref[...] = jnp.zeros_like(o_ref)

    hn = hn_ref[...]
    gate = jnp.dot(hn, wg_ref[...], preferred_element_type=F32)
    up = jnp.dot(hn, wu_ref[...], preferred_element_type=F32)
    h = (gate * jax.nn.sigmoid(gate) * up).astype(BF16)
    tn = _pick(o_ref.shape[1], wg_ref.shape[1])
    for c in range(o_ref.shape[1] // tn):
        cols = slice(c * tn, (c + 1) * tn)
        o_ref[:, cols] += jnp.dot(h, wd_ref[:, cols], preferred_element_type=F32)

    @pl.when(f == pl.num_programs(1) - 1)
    def _():
        x_copy.wait()
        y = x_scr[...] + o_ref[...]
        if final_norm:
            ms = jnp.mean(y * y, axis=-1, keepdims=True)
            y = y * lax.rsqrt(ms + EPS) * gf_ref[...]
        o_ref[...] = y


def _ffn(hn, x, w_gate_up, w_down, g_final, tm, tf, final_norm):
    s, d = x.shape
    d_ff = w_down.shape[0]
    nf = d_ff // tf
    return pl.pallas_call(
        functools.partial(_ffn_kernel, final_norm=final_norm),
        grid=(s // tm, nf),
        in_specs=[pl.BlockSpec((tm, d), lambda i, f: (i, 0)),
                  pl.BlockSpec(memory_space=pl.ANY),
                  pl.BlockSpec((d, tf), lambda i, f: (0, f)),
                  pl.BlockSpec((d, tf), lambda i, f: (0, nf + f)),
                  pl.BlockSpec((tf, d), lambda i, f: (f, 0)),
                  pl.BlockSpec((1, d), lambda i, f: (0, 0))],
        out_specs=pl.BlockSpec((tm, d), lambda i, f: (i, 0)),
        out_shape=jax.ShapeDtypeStruct((s, d), F32),
        scratch_shapes=[pltpu.VMEM((tm, d), F32), pltpu.SemaphoreType.DMA(())],
        compiler_params=_params("arbitrary", "arbitrary"),
        name="ffn",
    )(hn, x, w_gate_up, w_gate_up, w_down, g_final.reshape(1, d))


def _gated_merge_kernel(xn_ref, a_ref, b_ref, x_ref, *refs):
    o_ref, hn_ref = refs[-2:]
    bg_ref, wa_ref, wb_ref, wo_ref, gn_ref = refs[-7:-2]
    wg_refs = refs[:-7]
    nc = len(wg_refs) // 2
    d = x_ref.shape[1]
    tn = d // nc
    xn = xn_ref[...]
    a = a_ref[...]
    b = b_ref[...]
    parts = []
    for c in range(nc):
        cols = slice(c * tn, (c + 1) * tn)
        g_a = jax.nn.sigmoid(jnp.dot(xn, wg_refs[c][...], preferred_element_type=F32)
                             + bg_ref[:, cols])
        g_b = jax.nn.sigmoid(jnp.dot(xn, wg_refs[nc + c][...], preferred_element_type=F32)
                             + bg_ref[:, d + c * tn:d + (c + 1) * tn])
        y_a = jnp.dot(a, wa_ref[:, cols], preferred_element_type=F32)
        y_b = jnp.dot(b, wb_ref[:, cols], preferred_element_type=F32)
        parts.append((g_a * y_a + g_b * y_b).astype(BF16))
    m = jnp.concatenate(parts, axis=1)
    y = x_ref[...] + jnp.dot(m, wo_ref[...], preferred_element_type=F32)
    o_ref[...] = y
    ms = jnp.mean(y * y, axis=-1, keepdims=True)
    hn_ref[...] = (y * lax.rsqrt(ms + EPS) * gn_ref[...]).astype(BF16)


def _gated_merge(xn, a, b, x, w_in, col0, b_gate, w_a, w_b, w_o, g_norm, tm, tn):
    s, d = x.shape
    wa = a.shape[1]
    wb = b.shape[1]
    j0 = col0 // tn
    resident = dict(pipeline_mode=pl.Buffered(1))
    gate_specs = [pl.BlockSpec((d, tn), lambda i, k=k: (0, j0 + k), **resident)
                  for k in range(2 * d // tn)]
    return pl.pallas_call(
        _gated_merge_kernel,
        grid=(s // tm,),
        in_specs=[pl.BlockSpec((tm, d), lambda i: (i, 0)),
                  pl.BlockSpec((tm, wa), lambda i: (i, 0)),
                  pl.BlockSpec((tm, wb), lambda i: (i, 0)),
                  pl.BlockSpec((tm, d), lambda i: (i, 0))]
        + gate_specs
        + [pl.BlockSpec((1, 2 * d), lambda i: (0, 0)),
           pl.BlockSpec((wa, d), lambda i: (0, 0), **resident),
           pl.BlockSpec((wb, d), lambda i: (0, 0), **resident),
           pl.BlockSpec((d, d), lambda i: (0, 0), **resident),
           pl.BlockSpec((1, d), lambda i: (0, 0))],
        out_specs=[pl.BlockSpec((tm, d), lambda i: (i, 0)),
                   pl.BlockSpec((tm, d), lambda i: (i, 0))],
        out_shape=[jax.ShapeDtypeStruct((s, d), F32),
                   jax.ShapeDtypeStruct((s, d), BF16)],
        compiler_params=_params("arbitrary"),
        name="gated_merge",
    )(xn, a, b, x, *([w_in] * (2 * d // tn)), b_gate.reshape(1, 2 * d), w_a, w_b, w_o,
      g_norm.reshape(1, d))


def _pick(n, pref):
    return pref if n % pref == 0 else n


def kernel(x, norm_mix, w_in, b_gate, ln_v_g, ln_v_b, w_s, b_s, w_proj_a, w_proj_b,
           w_out, norm_ffn, w_gate_up, w_down, norm_final):
    batch, seq, d = x.shape
    depth = w_in.shape[0]
    sgu_w = SGU_GROUPS * SGU_GROUP_DIM
    sb_w = SB_HEADS * SB_HEAD_DIM
    assert sgu_w == sb_w
    off_q = 2 * sgu_w
    off_g = off_q + 3 * sb_w
    outs = []
    for b in range(batch):
        h = x.reshape(seq, d) if batch == 1 else x[b]
        for l in range(depth):
            w_in_l = w_in[l].astype(BF16)
            xn, a, qkv = _sgu(h, norm_mix[l], w_in_l, ln_v_g[l], ln_v_b[l], w_s[l], b_s[l],
                              _pick(seq, 512))
            o_b = _attention(qkv)
            h, hn = _gated_merge(xn, a, o_b, h, w_in_l, off_g, b_gate[l],
                                 w_proj_a[l].astype(BF16), w_proj_b[l].astype(BF16),
                                 w_out[l].astype(BF16), norm_ffn[l], _pick(seq, 256),
                                 _pick(d, 1024))
            h = _ffn(hn, h, w_gate_up[l].astype(BF16), w_down[l].astype(BF16),
                     norm_final, _pick(seq, 1024), _pick(w_down.shape[1], 512),
                     final_norm=(l == depth - 1))
        outs.append(h)
    return outs[0].reshape(x.shape) if batch == 1 else jnp.stack(outs)
```

```python
import functools
import math

import jax
import jax.numpy as jnp
from jax import lax
from jax.experimental import pallas as pl
from jax.experimental.pallas import tpu as pltpu

EPS = 1e-6
CHUNK = 128
SGU_GROUPS = 8
SGU_GROUP_DIM = 128
SB_HEADS = 8
SB_HEAD_DIM = 128
ATT_BLOCK = 256
BF16_SUBLANES = 16
STICK_USED_F32_ZERO = 104.0
VMEM_LIMIT_BYTES = 60 * 1024 * 1024

F32 = jnp.float32
BF16 = jnp.bfloat16


def _params(*sem):
    return pltpu.CompilerParams(dimension_semantics=sem, vmem_limit_bytes=VMEM_LIMIT_BYTES)


def _exp(x):
    return jnp.exp2(x * math.log2(math.e))


def _gelu(x):
    return 0.5 * x * (1.0 + lax.erf(x * math.sqrt(0.5)))


def _sgu_kernel(x_ref, gn_ref, w_ref, wq_ref, wk_ref, wv_ref, lng_ref, lnb_ref, ws_ref, bs_ref,
                xn_ref, o_ref, qkv_ref):
    width = o_ref.shape[1]
    x = x_ref[...]
    ms = jnp.mean(x * x, axis=-1, keepdims=True)
    xn = (x * lax.rsqrt(ms + EPS) * gn_ref[...]).astype(BF16)
    xn_ref[...] = xn
    v = _gelu(jnp.dot(xn, w_ref[:, width:], preferred_element_type=F32))
    qkv_ref[:, :width] = jnp.dot(xn, wq_ref[...], preferred_element_type=F32).astype(BF16)
    mu = jnp.mean(v, axis=-1, keepdims=True)
    cen = v - mu
    var = jnp.mean(cen * cen, axis=-1, keepdims=True)
    vln = (cen * lax.rsqrt(var + EPS) * lng_ref[...] + lnb_ref[...]).astype(BF16)
    u = _gelu(jnp.dot(xn, w_ref[:, :width], preferred_element_type=F32))
    qkv_ref[:, width:2 * width] = jnp.dot(xn, wk_ref[...], preferred_element_type=F32).astype(BF16)
    qkv_ref[:, 2 * width:] = jnp.dot(xn, wv_ref[...], preferred_element_type=F32).astype(BF16)
    t_idx = lax.broadcasted_iota(jnp.int32, (CHUNK, CHUNK), 0)
    s_idx = lax.broadcasted_iota(jnp.int32, (CHUNK, CHUNK), 1)
    causal = s_idx <= t_idx
    for g in range(SGU_GROUPS):
        wm = jnp.where(causal, ws_ref[g], 0.0).astype(BF16)
        cols = slice(g * SGU_GROUP_DIM, (g + 1) * SGU_GROUP_DIM)
        for c in range(o_ref.shape[0] // CHUNK):
            rows = slice(c * CHUNK, (c + 1) * CHUNK)
            mixed = jnp.dot(wm, vln[rows, cols], preferred_element_type=F32) + bs_ref[g]
            o_ref[rows, cols] = (u[rows, cols] * mixed).astype(o_ref.dtype)


def _sgu(x, g_norm, w_in, ln_g, ln_b, w_s, b_s, tm):
    s, d = x.shape
    width = SGU_GROUPS * SGU_GROUP_DIM
    resident = dict(pipeline_mode=pl.Buffered(1))
    return pl.pallas_call(
        _sgu_kernel,
        grid=(s // tm,),
        in_specs=[pl.BlockSpec((tm, d), lambda i: (i, 0)),
                  pl.BlockSpec((1, d), lambda i: (0, 0)),
                  pl.BlockSpec((d, 2 * width), lambda i: (0, 0), **resident),
                  pl.BlockSpec((d, width), lambda i: (0, 2), **resident),
                  pl.BlockSpec((d, width), lambda i: (0, 3), **resident),
                  pl.BlockSpec((d, width), lambda i: (0, 4), **resident),
                  pl.BlockSpec((1, width), lambda i: (0, 0)),
                  pl.BlockSpec((1, width), lambda i: (0, 0)),
                  pl.BlockSpec(w_s.shape, lambda i: (0, 0, 0)),
                  pl.BlockSpec((SGU_GROUPS, CHUNK, 1), lambda i: (0, 0, 0))],
        out_specs=[pl.BlockSpec((tm, d), lambda i: (i, 0)),
                   pl.BlockSpec((tm, width), lambda i: (i, 0)),
                   pl.BlockSpec((tm, 3 * width), lambda i: (i, 0))],
        out_shape=[jax.ShapeDtypeStruct((s, d), BF16),
                   jax.ShapeDtypeStruct((s, width), BF16),
                   jax.ShapeDtypeStruct((s, 3 * width), BF16)],
        compiler_params=_params("arbitrary"),
        name="sgu_qkv",
    )(x, g_norm.reshape(1, d), w_in, w_in, w_in, w_in, ln_g.reshape(1, width),
      ln_b.reshape(1, width), w_s, b_s.reshape(SGU_GROUPS, CHUNK, 1))


def _attn_kernel(*refs, cast_every):
    n_cast = len(cast_every)
    q_ref, kd_ref, kp_ref, vd_ref, vp_ref, qkv_hbm = refs[:6]
    cast_in = refs[6:6 + n_cast]
    o_ref = refs[6 + n_cast]
    cast_out = refs[7 + n_cast:7 + 2 * n_cast]
    acc_ref, used_ref, kbuf, vbuf, sems = refs[7 + 2 * n_cast:]
    t = q_ref.shape[0]
    dh = SB_HEAD_DIM
    width = q_ref.shape[1]
    heads = width // dh
    qi = pl.program_id(0)
    scale = 1.0 / math.sqrt(dh)
    row = lax.broadcasted_iota(jnp.int32, (t, t), 0)
    col = lax.broadcasted_iota(jnp.int32, (t, t), 1)
    later = (row > col).astype(BF16)
    causal = col < row
    has_prev = qi > 0

    for src, dst, every in zip(cast_in, cast_out, cast_every):
        if every == 1:
            dst[...] = src[...].astype(BF16)
        else:
            @pl.when(qi % every == 0)
            def _(src=src, dst=dst):
                dst[...] = src[...].astype(BF16)

    def block(h, k_ref, mask):
        cols = slice(h * dh, (h + 1) * dh)
        z = lax.dot_general(q_ref[:, cols], k_ref[:, cols], (((1,), (1,)), ((), ())),
                            preferred_element_type=F32) * scale
        used = jnp.maximum(z, 0.0) + jnp.log(1.0 + _exp(-jnp.abs(z)))
        log_beta = z - used
        if mask is not None:
            used = jnp.where(mask, used, 0.0)
        after = jnp.dot(used.astype(BF16), later, preferred_element_type=F32)
        return log_beta - after, jnp.sum(used, axis=-1, keepdims=True)

    def pv(h, v_ref, w):
        cols = slice(h * dh, (h + 1) * dh)
        return jnp.dot(w.astype(BF16), v_ref[:, cols], preferred_element_type=F32)

    low = None
    for h in range(heads):
        cols = slice(h * dh, (h + 1) * dh)
        logw_d, used_d = block(h, kd_ref, causal)
        acc = pv(h, vd_ref, jnp.where(causal, _exp(logw_d), 0.0))
        logw_p, used_p = block(h, kp_ref, None)
        acc_p = pv(h, vp_ref, _exp(logw_p - used_d))
        acc_ref[:, cols] = acc + jnp.where(has_prev, acc_p, 0.0)
        used = used_d + used_p
        used_ref[h] = used
        low = jnp.min(used) if low is None else jnp.minimum(low, jnp.min(used))

    def cond(state):
        kb, low = state
        return jnp.logical_and(kb >= 0, low <= STICK_USED_F32_ZERO)

    def body(state):
        kb, _ = state
        rows = pl.ds(pl.multiple_of(kb * t, t), t)
        k_copy = pltpu.make_async_copy(qkv_hbm.at[rows, pl.ds(width, width)], kbuf, sems.at[0])
        v_copy = pltpu.make_async_copy(qkv_hbm.at[rows, pl.ds(2 * width, width)], vbuf, sems.at[1])
        k_copy.start()
        v_copy.start()
        k_copy.wait()
        v_copy.wait()
        low = None
        for h in range(heads):
            cols = slice(h * dh, (h + 1) * dh)
            logw, used_b = block(h, kbuf, None)
            used = used_ref[h]
            acc_ref[:, cols] += pv(h, vbuf, _exp(logw - used))
            used = used + used_b
            used_ref[h] = used
            low = jnp.min(used) if low is None else jnp.minimum(low, jnp.min(used))
        return kb - 1, low

    lax.while_loop(cond, body, (qi - 2, low))
    o_ref[...] = acc_ref[...].astype(o_ref.dtype)


def _cast_every(rows, steps):
    every = 1
    while (rows * every) % (steps * BF16_SUBLANES) and every < steps:
        every *= 2
    return every


def _attention(qkv, weights):
    s = qkv.shape[0]
    t = ATT_BLOCK
    w = SB_HEADS * SB_HEAD_DIM
    steps = s // t
    prev = lambda i: jnp.maximum(i - 1, 0)
    cast_every = tuple(_cast_every(wt.shape[0], steps) for wt in weights)
    cast_specs = [pl.BlockSpec((wt.shape[0] * e // steps, wt.shape[1]), lambda i, e=e: (i // e, 0))
                  for wt, e in zip(weights, cast_every)]
    outs = pl.pallas_call(
        functools.partial(_attn_kernel, cast_every=cast_every),
        grid=(steps,),
        in_specs=[pl.BlockSpec((t, w), lambda i: (i, 0)),
                  pl.BlockSpec((t, w), lambda i: (i, 1)),
                  pl.BlockSpec((t, w), lambda i: (prev(i), 1)),
                  pl.BlockSpec((t, w), lambda i: (i, 2)),
                  pl.BlockSpec((t, w), lambda i: (prev(i), 2)),
                  pl.BlockSpec(memory_space=pl.ANY)] + cast_specs,
        out_specs=[pl.BlockSpec((t, w), lambda i: (i, 0))] + cast_specs,
        out_shape=[jax.ShapeDtypeStruct((s, w), BF16)]
        + [jax.ShapeDtypeStruct(wt.shape, BF16) for wt in weights],
        scratch_shapes=[pltpu.VMEM((t, w), F32), pltpu.VMEM((SB_HEADS, t, 1), F32),
                        pltpu.VMEM((t, w), BF16), pltpu.VMEM((t, w), BF16),
                        pltpu.SemaphoreType.DMA((2,))],
        compiler_params=_params("arbitrary"),
        name="sb_attention",
    )(qkv, qkv, qkv, qkv, qkv, qkv, *weights)
    return outs[0], outs[1:]


def _ffn_kernel(hn_ref, x_hbm, wg_ref, wu_ref, wd_ref, gf_ref, o_ref, x_scr, x_sem, *, final_norm):
    i = pl.program_id(0)
    f = pl.program_id(1)
    tm = o_ref.shape[0]
    x_copy = pltpu.make_async_copy(x_hbm.at[pl.ds(pl.multiple_of(i * tm, tm), tm), :], x_scr, x_sem)

    @pl.when(f == 0)
    def _():
        x_copy.start()
        o_ref[...] = jnp.zeros_like(o_ref)

    hn = hn_ref[...]
    gate = jnp.dot(hn, wg_ref[...], preferred_element_type=F32)
    up = jnp.dot(hn, wu_ref[...], preferred_element_type=F32)
    h = (gate * jax.nn.sigmoid(gate) * up).astype(BF16)
    tn = _pick(o_ref.shape[1], wg_ref.shape[1])
    for c in range(o_ref.shape[1] // tn):
        cols = slice(c * tn, (c + 1) * tn)
        o_ref[:, cols] += jnp.dot(h, wd_ref[:, cols], preferred_element_type=F32)

    @pl.when(f == pl.num_programs(1) - 1)
    def _():
        x_copy.wait()
        y = x_scr[...] + o_ref[...]
        if final_norm:
            ms = jnp.mean(y * y, axis=-1, keepdims=True)
            y = y * lax.rsqrt(ms + EPS) * gf_ref[...]
        o_ref[...] = y


def _ffn(hn, x, w_gate_up, w_down, g_final, tm, tf, final_norm):
    s, d = x.shape
    d_ff = w_down.shape[0]
    nf = d_ff // tf
    return pl.pallas_call(
        functools.partial(_ffn_kernel, final_norm=final_norm),
        grid=(s // tm, nf),
        in_specs=[pl.BlockSpec((tm, d), lambda i, f: (i, 0)),
                  pl.BlockSpec(memory_space=pl.ANY),
                  pl.BlockSpec((d, tf), lambda i, f: (0, f)),
                  pl.BlockSpec((d, tf), lambda i, f: (0, nf + f)),
                  pl.BlockSpec((tf, d), lambda i, f: (f, 0)),
                  pl.BlockSpec((1, d), lambda i, f: (0, 0))],
        out_specs=pl.BlockSpec((tm, d), lambda i, f: (i, 0)),
        out_shape=jax.ShapeDtypeStruct((s, d), F32),
        scratch_shapes=[pltpu.VMEM((tm, d), F32), pltpu.SemaphoreType.DMA(())],
        compiler_params=_params("arbitrary", "arbitrary"),
        name="ffn",
    )(hn, x, w_gate_up, w_gate_up, w_down, g_final.reshape(1, d))


def _gated_merge_kernel(xn_ref, a_ref, b_ref, x_ref, *refs):
    o_ref, hn_ref = refs[-2:]
    bg_ref, wa_ref, wb_ref, wo_ref, gn_ref = refs[-7:-2]
    wg_refs = refs[:-7]
    nc = len(wg_refs) // 2
    d = x_ref.shape[1]
    tn = d // nc
    xn = xn_ref[...]
    a = a_ref[...]
    b = b_ref[...]
    parts = []
    for c in range(nc):
        cols = slice(c * tn, (c + 1) * tn)
        g_a = jax.nn.sigmoid(jnp.dot(xn, wg_refs[c][...], preferred_element_type=F32)
                             + bg_ref[:, cols])
        g_b = jax.nn.sigmoid(jnp.dot(xn, wg_refs[nc + c][...], preferred_element_type=F32)
                             + bg_ref[:, d + c * tn:d + (c + 1) * tn])
        y_a = jnp.dot(a, wa_ref[:, cols], preferred_element_type=F32)
        y_b = jnp.dot(b, wb_ref[:, cols], preferred_element_type=F32)
        parts.append((g_a * y_a + g_b * y_b).astype(BF16))
    m = jnp.concatenate(parts, axis=1)
    y = x_ref[...] + jnp.dot(m, wo_ref[...], preferred_element_type=F32)
    o_ref[...] = y
    ms = jnp.mean(y * y, axis=-1, keepdims=True)
    hn_ref[...] = (y * lax.rsqrt(ms + EPS) * gn_ref[...]).astype(BF16)


def _gated_merge(xn, a, b, x, w_in, col0, b_gate, w_a, w_b, w_o, g_norm, tm, tn):
    s, d = x.shape
    wa = a.shape[1]
    wb = b.shape[1]
    j0 = col0 // tn
    resident = dict(pipeline_mode=pl.Buffered(1))
    gate_specs = [pl.BlockSpec((d, tn), lambda i, k=k: (0, j0 + k), **resident)
                  for k in range(2 * d // tn)]
    return pl.pallas_call(
        _gated_merge_kernel,
        grid=(s // tm,),
        in_specs=[pl.BlockSpec((tm, d), lambda i: (i, 0)),
                  pl.BlockSpec((tm, wa), lambda i: (i, 0)),
                  pl.BlockSpec((tm, wb), lambda i: (i, 0)),
                  pl.BlockSpec((tm, d), lambda i: (i, 0))]
        + gate_specs
        + [pl.BlockSpec((1, 2 * d), lambda i: (0, 0)),
           pl.BlockSpec((wa, d), lambda i: (0, 0), **resident),
           pl.BlockSpec((wb, d), lambda i: (0, 0), **resident),
           pl.BlockSpec((d, d), lambda i: (0, 0), **resident),
           pl.BlockSpec((1, d), lambda i: (0, 0))],
        out_specs=[pl.BlockSpec((tm, d), lambda i: (i, 0)),
                   pl.BlockSpec((tm, d), lambda i: (i, 0))],
        out_shape=[jax.ShapeDtypeStruct((s, d), F32),
                   jax.ShapeDtypeStruct((s, d), BF16)],
        compiler_params=_params("arbitrary"),
        name="gated_merge",
    )(xn, a, b, x, *([w_in] * (2 * d // tn)), b_gate.reshape(1, 2 * d), w_a, w_b, w_o,
      g_norm.reshape(1, d))


def _pick(n, pref):
    return pref if n % pref == 0 else n


def kernel(x, norm_mix, w_in, b_gate, ln_v_g, ln_v_b, w_s, b_s, w_proj_a, w_proj_b,
           w_out, norm_ffn, w_gate_up, w_down, norm_final):
    batch, seq, d = x.shape
    depth = w_in.shape[0]
    sgu_w = SGU_GROUPS * SGU_GROUP_DIM
    sb_w = SB_HEADS * SB_HEAD_DIM
    assert sgu_w == sb_w
    off_q = 2 * sgu_w
    off_g = off_q + 3 * sb_w
    outs = []
    for b in range(batch):
        h = x.reshape(seq, d) if batch == 1 else x[b]
        for l in range(depth):
            w_in_l = w_in[l].astype(BF16)
            xn, a, qkv = _sgu(h, norm_mix[l], w_in_l, ln_v_g[l], ln_v_b[l], w_s[l], b_s[l],
                              _pick(seq, 512))
            o_b, (wa_l, wb_l, wo_l, wgu_l, wd_l) = _attention(
                qkv, [w_proj_a[l], w_proj_b[l], w_out[l], w_gate_up[l], w_down[l]])
            h, hn = _gated_merge(xn, a, o_b, h, w_in_l, off_g, b_gate[l], wa_l, wb_l, wo_l,
                                 norm_ffn[l], _pick(seq, 256), _pick(d, 1024))
            h = _ffn(hn, h, wgu_l, wd_l, norm_final, _pick(seq, 1024),
                     _pick(w_down.shape[1], 512), final_norm=(l == depth - 1))
        outs.append(h)
    return outs[0].reshape(x.shape) if batch == 1 else jnp.stack(outs)
```

```python
import functools
import math

import jax
import jax.numpy as jnp
from jax import lax
from jax.experimental import pallas as pl
from jax.experimental.pallas import tpu as pltpu

EPS = 1e-6
CHUNK = 128
SGU_GROUPS = 8
SGU_GROUP_DIM = 128
SB_HEADS = 8
SB_HEAD_DIM = 128
ATT_BLOCK = 256
BF16_SUBLANES = 16
FINAL_ROWS = 16
STICK_USED_F32_ZERO = 104.0
MASKED_SCORE = -1e30
VMEM_LIMIT_BYTES = 60 * 1024 * 1024

F32 = jnp.float32
BF16 = jnp.bfloat16


def _params(*sem):
    return pltpu.CompilerParams(dimension_semantics=sem, vmem_limit_bytes=VMEM_LIMIT_BYTES)


def _exp(x):
    return jnp.exp2(x * math.log2(math.e))


def _gelu(x):
    return 0.5 * x * (1.0 + lax.erf(x * math.sqrt(0.5)))


def _sgu_kernel(x_ref, gn_ref, w_ref, wq_ref, wk_ref, wv_ref, lng_ref, lnb_ref, ws_ref, bs_ref,
                xn_ref, o_ref, qkv_ref):
    width = o_ref.shape[1]
    x = x_ref[...]
    ms = jnp.mean(x * x, axis=-1, keepdims=True)
    xn = (x * lax.rsqrt(ms + EPS) * gn_ref[...]).astype(BF16)
    xn_ref[...] = xn
    v = _gelu(jnp.dot(xn, w_ref[:, width:], preferred_element_type=F32))
    qkv_ref[:, :width] = jnp.dot(xn, wq_ref[...], preferred_element_type=F32).astype(BF16)
    mu = jnp.mean(v, axis=-1, keepdims=True)
    cen = v - mu
    var = jnp.mean(cen * cen, axis=-1, keepdims=True)
    vln = (cen * lax.rsqrt(var + EPS) * lng_ref[...] + lnb_ref[...]).astype(BF16)
    u = _gelu(jnp.dot(xn, w_ref[:, :width], preferred_element_type=F32))
    qkv_ref[:, width:2 * width] = jnp.dot(xn, wk_ref[...], preferred_element_type=F32).astype(BF16)
    qkv_ref[:, 2 * width:] = jnp.dot(xn, wv_ref[...], preferred_element_type=F32).astype(BF16)
    t_idx = lax.broadcasted_iota(jnp.int32, (CHUNK, CHUNK), 0)
    s_idx = lax.broadcasted_iota(jnp.int32, (CHUNK, CHUNK), 1)
    causal = s_idx <= t_idx
    for g in range(SGU_GROUPS):
        wm = jnp.where(causal, ws_ref[g], 0.0).astype(BF16)
        cols = slice(g * SGU_GROUP_DIM, (g + 1) * SGU_GROUP_DIM)
        for c in range(o_ref.shape[0] // CHUNK):
            rows = slice(c * CHUNK, (c + 1) * CHUNK)
            mixed = jnp.dot(wm, vln[rows, cols], preferred_element_type=F32) + bs_ref[g]
            o_ref[rows, cols] = (u[rows, cols] * mixed).astype(o_ref.dtype)


def _sgu(x, g_norm, w_in, ln_g, ln_b, w_s, b_s, tm):
    s, d = x.shape
    width = SGU_GROUPS * SGU_GROUP_DIM
    resident = dict(pipeline_mode=pl.Buffered(1))
    return pl.pallas_call(
        _sgu_kernel,
        grid=(s // tm,),
        in_specs=[pl.BlockSpec((tm, d), lambda i: (i, 0)),
                  pl.BlockSpec((1, d), lambda i: (0, 0)),
                  pl.BlockSpec((d, 2 * width), lambda i: (0, 0), **resident),
                  pl.BlockSpec((d, width), lambda i: (0, 2), **resident),
                  pl.BlockSpec((d, width), lambda i: (0, 3), **resident),
                  pl.BlockSpec((d, width), lambda i: (0, 4), **resident),
                  pl.BlockSpec((1, width), lambda i: (0, 0)),
                  pl.BlockSpec((1, width), lambda i: (0, 0)),
                  pl.BlockSpec(w_s.shape, lambda i: (0, 0, 0)),
                  pl.BlockSpec((SGU_GROUPS, CHUNK, 1), lambda i: (0, 0, 0))],
        out_specs=[pl.BlockSpec((tm, d), lambda i: (i, 0)),
                   pl.BlockSpec((tm, width), lambda i: (i, 0)),
                   pl.BlockSpec((tm, 3 * width), lambda i: (i, 0))],
        out_shape=[jax.ShapeDtypeStruct((s, d), BF16),
                   jax.ShapeDtypeStruct((s, width), BF16),
                   jax.ShapeDtypeStruct((s, 3 * width), BF16)],
        compiler_params=_params("arbitrary"),
        name="sgu_qkv",
    )(x, g_norm.reshape(1, d), w_in, w_in, w_in, w_in, ln_g.reshape(1, width),
      ln_b.reshape(1, width), w_s, b_s.reshape(SGU_GROUPS, CHUNK, 1))


def _attn_kernel(*refs, cast_every):
    n_cast = len(cast_every)
    q_ref, kd_ref, kp_ref, vd_ref, vp_ref, qkv_hbm = refs[:6]
    cast_in = refs[6:6 + n_cast]
    o_ref = refs[6 + n_cast]
    cast_out = refs[7 + n_cast:7 + 2 * n_cast]
    acc_ref, used_ref, kbuf, vbuf, sems = refs[7 + 2 * n_cast:]
    t = q_ref.shape[0]
    dh = SB_HEAD_DIM
    width = q_ref.shape[1]
    heads = width // dh
    qi = pl.program_id(0)
    scale = 1.0 / math.sqrt(dh)
    row = lax.broadcasted_iota(jnp.int32, (t, t), 0)
    col = lax.broadcasted_iota(jnp.int32, (t, t), 1)
    later = (row > col).astype(BF16)
    causal = col < row
    has_prev = qi > 0

    for src, dst, every in zip(cast_in, cast_out, cast_every):
        if every == 1:
            dst[...] = src[...].astype(BF16)
        else:
            @pl.when(qi % every == 0)
            def _(src=src, dst=dst):
                dst[...] = src[...].astype(BF16)

    def block(h, k_ref, mask):
        cols = slice(h * dh, (h + 1) * dh)
        z = lax.dot_general(q_ref[:, cols], k_ref[:, cols], (((1,), (1,)), ((), ())),
                            preferred_element_type=F32) * scale
        if mask is not None:
            z = jnp.where(mask, z, MASKED_SCORE)
        used = jnp.maximum(z, 0.0) + jnp.log(1.0 + _exp(-jnp.abs(z)))
        log_beta = z - used
        after = jnp.dot(used.astype(BF16), later, preferred_element_type=F32)
        return log_beta - after, jnp.sum(used, axis=-1, keepdims=True)

    def pv(h, v_ref, w):
        cols = slice(h * dh, (h + 1) * dh)
        return jnp.dot(w.astype(BF16), v_ref[:, cols], preferred_element_type=F32)

    least = None
    for h in range(heads):
        cols = slice(h * dh, (h + 1) * dh)
        logw_d, used_d = block(h, kd_ref, causal)
        acc = pv(h, vd_ref, _exp(logw_d))
        logw_p, used_p = block(h, kp_ref, None)
        acc_p = pv(h, vp_ref, _exp(logw_p - used_d))
        acc_ref[:, cols] = acc + jnp.where(has_prev, acc_p, 0.0)
        used = used_d + used_p
        used_ref[h] = used
        least = used if least is None else jnp.minimum(least, used)
    low = jnp.min(least)

    def cond(state):
        kb, low = state
        return jnp.logical_and(kb >= 0, low <= STICK_USED_F32_ZERO)

    def body(state):
        kb, _ = state
        rows = pl.ds(pl.multiple_of(kb * t, t), t)
        k_copy = pltpu.make_async_copy(qkv_hbm.at[rows, pl.ds(width, width)], kbuf, sems.at[0])
        v_copy = pltpu.make_async_copy(qkv_hbm.at[rows, pl.ds(2 * width, width)], vbuf, sems.at[1])
        k_copy.start()
        v_copy.start()
        k_copy.wait()
        v_copy.wait()
        least = None
        for h in range(heads):
            cols = slice(h * dh, (h + 1) * dh)
            logw, used_b = block(h, kbuf, None)
            used = used_ref[h]
            acc_ref[:, cols] += pv(h, vbuf, _exp(logw - used))
            used = used + used_b
            used_ref[h] = used
            least = used if least is None else jnp.minimum(least, used)
        return kb - 1, jnp.min(least)

    lax.while_loop(cond, body, (qi - 2, low))
    o_ref[...] = acc_ref[...].astype(o_ref.dtype)


def _cast_every(rows, steps):
    every = 1
    while (rows * every) % (steps * BF16_SUBLANES) and every < steps:
        every *= 2
    return every


def _attention(qkv, weights):
    s = qkv.shape[0]
    t = ATT_BLOCK
    w = SB_HEADS * SB_HEAD_DIM
    steps = s // t
    prev = lambda i: jnp.maximum(i - 1, 0)
    cast_every = tuple(_cast_every(wt.shape[0], steps) for wt, _, _ in weights)
    cast_in_specs = [pl.BlockSpec((wt.shape[0] * e // steps, nc), lambda i, e=e, cb=cb: (i // e, cb))
                     for (wt, nc, cb), e in zip(weights, cast_every)]
    cast_out_specs = [pl.BlockSpec((wt.shape[0] * e // steps, nc), lambda i, e=e: (i // e, 0))
                      for (wt, nc, _), e in zip(weights, cast_every)]
    outs = pl.pallas_call(
        functools.partial(_attn_kernel, cast_every=cast_every),
        grid=(steps,),
        in_specs=[pl.BlockSpec((t, w), lambda i: (i, 0)),
                  pl.BlockSpec((t, w), lambda i: (i, 1)),
                  pl.BlockSpec((t, w), lambda i: (prev(i), 1)),
                  pl.BlockSpec((t, w), lambda i: (i, 2)),
                  pl.BlockSpec((t, w), lambda i: (prev(i), 2)),
                  pl.BlockSpec(memory_space=pl.ANY)] + cast_in_specs,
        out_specs=[pl.BlockSpec((t, w), lambda i: (i, 0))] + cast_out_specs,
        out_shape=[jax.ShapeDtypeStruct((s, w), BF16)]
        + [jax.ShapeDtypeStruct((wt.shape[0], nc), BF16) for wt, nc, _ in weights],
        scratch_shapes=[pltpu.VMEM((t, w), F32), pltpu.VMEM((SB_HEADS, t, 1), F32),
                        pltpu.VMEM((t, w), BF16), pltpu.VMEM((t, w), BF16),
                        pltpu.SemaphoreType.DMA((2,))],
        compiler_params=_params("arbitrary"),
        name="sb_attention",
    )(qkv, qkv, qkv, qkv, qkv, qkv, *[wt for wt, _, _ in weights])
    return outs[0], outs[1:]


def _ffn_kernel(hn_ref, x_hbm, wg_ref, wu_ref, wd_ref, gf_ref, o_ref, x_scr, x_sem, *, final_norm):
    i = pl.program_id(0)
    f = pl.program_id(1)
    tm = o_ref.shape[0]
    x_copy = pltpu.make_async_copy(x_hbm.at[pl.ds(pl.multiple_of(i * tm, tm), tm), :], x_scr, x_sem)

    @pl.when(f == 0)
    def _():
        x_copy.start()
        o_ref[...] = jnp.zeros_like(o_ref)

    hn = hn_ref[...]
    gate = jnp.dot(hn, wg_ref[...], preferred_element_type=F32)
    up = jnp.dot(hn, wu_ref[...], preferred_element_type=F32)
    h = (gate * jax.nn.sigmoid(gate) * up).astype(BF16)
    tn = _pick(o_ref.shape[1], wg_ref.shape[1])
    for c in range(o_ref.shape[1] // tn):
        cols = slice(c * tn, (c + 1) * tn)
        o_ref[:, cols] += jnp.dot(h, wd_ref[:, cols], preferred_element_type=F32)

    @pl.when(f == pl.num_programs(1) - 1)
    def _():
        x_copy.wait()
        for r in range(0, tm, FINAL_ROWS):
            rows = slice(r, r + FINAL_ROWS)
            y = x_scr[rows, :] + o_ref[rows, :]
            if final_norm:
                ms = jnp.mean(y * y, axis=-1, keepdims=True)
                y = y * lax.rsqrt(ms + EPS) * gf_ref[...]
            o_ref[rows, :] = y


def _ffn(hn, x, w_gate_up, w_down, g_final, tm, tf, final_norm):
    s, d = x.shape
    d_ff = w_down.shape[0]
    nf = d_ff // tf
    return pl.pallas_call(
        functools.partial(_ffn_kernel, final_norm=final_norm),
        grid=(s // tm, nf),
        in_specs=[pl.BlockSpec((tm, d), lambda i, f: (i, 0)),
                  pl.BlockSpec(memory_space=pl.ANY),
                  pl.BlockSpec((d, tf), lambda i, f: (0, f)),
                  pl.BlockSpec((d, tf), lambda i, f: (0, nf + f)),
                  pl.BlockSpec((tf, d), lambda i, f: (f, 0)),
                  pl.BlockSpec((1, d), lambda i, f: (0, 0))],
        out_specs=pl.BlockSpec((tm, d), lambda i, f: (i, 0)),
        out_shape=jax.ShapeDtypeStruct((s, d), F32),
        scratch_shapes=[pltpu.VMEM((tm, d), F32), pltpu.SemaphoreType.DMA(())],
        compiler_params=_params("arbitrary", "arbitrary"),
        name="ffn",
    )(hn, x, w_gate_up, w_gate_up, w_down, g_final.reshape(1, d))


def _gated_merge_kernel(xn_ref, a_ref, b_ref, x_ref, *refs):
    o_ref, hn_ref = refs[-2:]
    bg_ref, wa_ref, wb_ref, wo_ref, gn_ref = refs[-7:-2]
    wg_refs = refs[:-7]
    nc = len(wg_refs) // 2
    d = x_ref.shape[1]
    tn = d // nc
    xn = xn_ref[...]
    a = a_ref[...]
    b = b_ref[...]
    parts = []
    for c in range(nc):
        cols = slice(c * tn, (c + 1) * tn)
        g_a = jax.nn.sigmoid(jnp.dot(xn, wg_refs[c][...], preferred_element_type=F32)
                             + bg_ref[:, cols])
        g_b = jax.nn.sigmoid(jnp.dot(xn, wg_refs[nc + c][...], preferred_element_type=F32)
                             + bg_ref[:, d + c * tn:d + (c + 1) * tn])
        y_a = jnp.dot(a, wa_ref[:, cols], preferred_element_type=F32)
        y_b = jnp.dot(b, wb_ref[:, cols], preferred_element_type=F32)
        parts.append((g_a * y_a + g_b * y_b).astype(BF16))
    m = jnp.concatenate(parts, axis=1)
    y = x_ref[...] + jnp.dot(m, wo_ref[...], preferred_element_type=F32)
    o_ref[...] = y
    ms = jnp.mean(y * y, axis=-1, keepdims=True)
    hn_ref[...] = (y * lax.rsqrt(ms + EPS) * gn_ref[...]).astype(BF16)


def _gated_merge(xn, a, b, x, gate_tiles, b_gate, w_a, w_b, w_o, g_norm, tm):
    s, d = x.shape
    wa = a.shape[1]
    wb = b.shape[1]
    resident = dict(pipeline_mode=pl.Buffered(1))
    gate_specs = [pl.BlockSpec(g.shape, lambda i: (0, 0), **resident) for g in gate_tiles]
    return pl.pallas_call(
        _gated_merge_kernel,
        grid=(s // tm,),
        in_specs=[pl.BlockSpec((tm, d), lambda i: (i, 0)),
                  pl.BlockSpec((tm, wa), lambda i: (i, 0)),
                  pl.BlockSpec((tm, wb), lambda i: (i, 0)),
                  pl.BlockSpec((tm, d), lambda i: (i, 0))]
        + gate_specs
        + [pl.BlockSpec((1, 2 * d), lambda i: (0, 0)),
           pl.BlockSpec((wa, d), lambda i: (0, 0), **resident),
           pl.BlockSpec((wb, d), lambda i: (0, 0), **resident),
           pl.BlockSpec((d, d), lambda i: (0, 0), **resident),
           pl.BlockSpec((1, d), lambda i: (0, 0))],
        out_specs=[pl.BlockSpec((tm, d), lambda i: (i, 0)),
                   pl.BlockSpec((tm, d), lambda i: (i, 0))],
        out_shape=[jax.ShapeDtypeStruct((s, d), F32),
                   jax.ShapeDtypeStruct((s, d), BF16)],
        compiler_params=_params("arbitrary"),
        name="gated_merge",
    )(xn, a, b, x, *gate_tiles, b_gate.reshape(1, 2 * d), w_a, w_b, w_o, g_norm.reshape(1, d))


def _pick(n, pref):
    return pref if n % pref == 0 else n


def kernel(x, norm_mix, w_in, b_gate, ln_v_g, ln_v_b, w_s, b_s, w_proj_a, w_proj_b,
           w_out, norm_ffn, w_gate_up, w_down, norm_final):
    batch, seq, d = x.shape
    depth = w_in.shape[0]
    sgu_w = SGU_GROUPS * SGU_GROUP_DIM
    sb_w = SB_HEADS * SB_HEAD_DIM
    assert sgu_w == sb_w
    off_q = 2 * sgu_w
    off_g = off_q + 3 * sb_w
    outs = []
    for b in range(batch):
        h = x.reshape(seq, d) if batch == 1 else x[b]
        for l in range(depth):
            w_first = w_in[l][:, :off_g].astype(BF16)
            xn, a, qkv = _sgu(h, norm_mix[l], w_first, ln_v_g[l], ln_v_b[l], w_s[l], b_s[l],
                              _pick(seq, 512))
            tn = _pick(d, 1024)
            whole = lambda wt: (wt, wt.shape[1], 0)
            gate_cols = [(w_in[l], tn, off_g // tn + k) for k in range(2 * d // tn)]
            o_b, bf = _attention(qkv, [whole(w_proj_a[l]), whole(w_proj_b[l]), whole(w_out[l]),
                                       whole(w_gate_up[l]), whole(w_down[l])] + gate_cols)
            wa_l, wb_l, wo_l, wgu_l, wd_l = bf[:5]
            h, hn = _gated_merge(xn, a, o_b, h, bf[5:], b_gate[l], wa_l, wb_l, wo_l,
                                 norm_ffn[l], _pick(seq, 256))
            h = _ffn(hn, h, wgu_l, wd_l, norm_final, _pick(seq, 1024),
                     _pick(w_down.shape[1], 512), final_norm=(l == depth - 1))
        outs.append(h)
    return outs[0].reshape(x.shape) if batch == 1 else jnp.stack(outs)
```

```python
import functools
import math

import jax
import jax.numpy as jnp
from jax import lax
from jax.experimental import pallas as pl
from jax.experimental.pallas import tpu as pltpu

EPS = 1e-6
CHUNK = 128
SGU_GROUPS = 8
SGU_GROUP_DIM = 128
SB_HEADS = 8
SB_HEAD_DIM = 128
ATT_BLOCK = 256
BF16_SUBLANES = 16
FINAL_ROWS = 16
STICK_USED_F32_ZERO = 104.0
MASKED_SCORE = -1e30
ATT_HEADS_PER_TRIANGLE = 2
VMEM_LIMIT_BYTES = 60 * 1024 * 1024

F32 = jnp.float32
BF16 = jnp.bfloat16


def _params(*sem):
    return pltpu.CompilerParams(dimension_semantics=sem, vmem_limit_bytes=VMEM_LIMIT_BYTES)


def _exp(x):
    return jnp.exp2(x * math.log2(math.e))


def _gelu(x):
    return 0.5 * x * (1.0 + lax.erf(x * math.sqrt(0.5)))


def _sgu_kernel(x_ref, gn_ref, w_ref, wq_ref, wk_ref, wv_ref, lng_ref, lnb_ref, ws_ref, bs_ref,
                xn_ref, o_ref, qkv_ref):
    width = o_ref.shape[1]
    x = x_ref[...]
    ms = jnp.mean(x * x, axis=-1, keepdims=True)
    xn = (x * lax.rsqrt(ms + EPS) * gn_ref[...]).astype(BF16)
    xn_ref[...] = xn
    v = _gelu(jnp.dot(xn, w_ref[:, width:], preferred_element_type=F32))
    qkv_ref[:, :width] = jnp.dot(xn, wq_ref[...], preferred_element_type=F32).astype(BF16)
    mu = jnp.mean(v, axis=-1, keepdims=True)
    cen = v - mu
    var = jnp.mean(cen * cen, axis=-1, keepdims=True)
    vln = (cen * lax.rsqrt(var + EPS) * lng_ref[...] + lnb_ref[...]).astype(BF16)
    u = _gelu(jnp.dot(xn, w_ref[:, :width], preferred_element_type=F32))
    qkv_ref[:, width:2 * width] = jnp.dot(xn, wk_ref[...], preferred_element_type=F32).astype(BF16)
    qkv_ref[:, 2 * width:] = jnp.dot(xn, wv_ref[...], preferred_element_type=F32).astype(BF16)
    t_idx = lax.broadcasted_iota(jnp.int32, (CHUNK, CHUNK), 0)
    s_idx = lax.broadcasted_iota(jnp.int32, (CHUNK, CHUNK), 1)
    causal = s_idx <= t_idx
    for g in range(SGU_GROUPS):
        wm = jnp.where(causal, ws_ref[g], 0.0).astype(BF16)
        cols = slice(g * SGU_GROUP_DIM, (g + 1) * SGU_GROUP_DIM)
        for c in range(o_ref.shape[0] // CHUNK):
            rows = slice(c * CHUNK, (c + 1) * CHUNK)
            mixed = jnp.dot(wm, vln[rows, cols], preferred_element_type=F32) + bs_ref[g]
            o_ref[rows, cols] = (u[rows, cols] * mixed).astype(o_ref.dtype)


def _sgu(x, g_norm, w_in, ln_g, ln_b, w_s, b_s, tm):
    s, d = x.shape
    width = SGU_GROUPS * SGU_GROUP_DIM
    resident = dict(pipeline_mode=pl.Buffered(1))
    return pl.pallas_call(
        _sgu_kernel,
        grid=(s // tm,),
        in_specs=[pl.BlockSpec((tm, d), lambda i: (i, 0)),
                  pl.BlockSpec((1, d), lambda i: (0, 0)),
                  pl.BlockSpec((d, 2 * width), lambda i: (0, 0), **resident),
                  pl.BlockSpec((d, width), lambda i: (0, 2), **resident),
                  pl.BlockSpec((d, width), lambda i: (0, 3), **resident),
                  pl.BlockSpec((d, width), lambda i: (0, 4), **resident),
                  pl.BlockSpec((1, width), lambda i: (0, 0)),
                  pl.BlockSpec((1, width), lambda i: (0, 0)),
                  pl.BlockSpec(w_s.shape, lambda i: (0, 0, 0)),
                  pl.BlockSpec((SGU_GROUPS, CHUNK, 1), lambda i: (0, 0, 0))],
        out_specs=[pl.BlockSpec((tm, d), lambda i: (i, 0)),
                   pl.BlockSpec((tm, width), lambda i: (i, 0)),
                   pl.BlockSpec((tm, 3 * width), lambda i: (i, 0))],
        out_shape=[jax.ShapeDtypeStruct((s, d), BF16),
                   jax.ShapeDtypeStruct((s, width), BF16),
                   jax.ShapeDtypeStruct((s, 3 * width), BF16)],
        compiler_params=_params("arbitrary"),
        name="sgu_qkv",
    )(x, g_norm.reshape(1, d), w_in, w_in, w_in, w_in, ln_g.reshape(1, width),
      ln_b.reshape(1, width), w_s, b_s.reshape(SGU_GROUPS, CHUNK, 1))


def _attn_kernel(*refs, cast_every):
    n_cast = len(cast_every)
    q_ref, kd_ref, kp_ref, vd_ref, vp_ref, qkv_hbm = refs[:6]
    cast_in = refs[6:6 + n_cast]
    o_ref = refs[6 + n_cast]
    cast_out = refs[7 + n_cast:7 + 2 * n_cast]
    acc_ref, used_ref, kbuf, vbuf, sems = refs[7 + 2 * n_cast:]
    t = q_ref.shape[0]
    dh = SB_HEAD_DIM
    width = q_ref.shape[1]
    heads = width // dh
    qi = pl.program_id(0)
    scale = 1.0 / math.sqrt(dh)
    row = lax.broadcasted_iota(jnp.int32, (t, t), 0)
    col = lax.broadcasted_iota(jnp.int32, (t, t), 1)
    later = (row > col).astype(BF16)
    causal = col < row
    no_prev = jnp.where(qi > 0, 0.0, -MASKED_SCORE)

    for src, dst, every in zip(cast_in, cast_out, cast_every):
        if every == 1:
            dst[...] = src[...].astype(BF16)
        else:
            @pl.when(qi % every == 0)
            def _(src=src, dst=dst):
                dst[...] = src[...].astype(BF16)

    def block(h, k_ref, mask):
        cols = slice(h * dh, (h + 1) * dh)
        z = lax.dot_general(q_ref[:, cols], k_ref[:, cols], (((1,), (1,)), ((), ())),
                            preferred_element_type=F32) * scale
        if mask is not None:
            z = jnp.where(mask, z, MASKED_SCORE)
        used = jnp.maximum(z, 0.0) + jnp.log(1.0 + _exp(-jnp.abs(z)))
        log_beta = z - used
        after = jnp.dot(used.astype(BF16), later, preferred_element_type=F32)
        return log_beta - after, jnp.sum(used, axis=-1, keepdims=True)

    def pv(h, v_ref, w):
        cols = slice(h * dh, (h + 1) * dh)
        return jnp.dot(w.astype(BF16), v_ref[:, cols], preferred_element_type=F32)

    least = None
    for h0 in range(0, heads, ATT_HEADS_PER_TRIANGLE):
        group = range(h0, h0 + ATT_HEADS_PER_TRIANGLE)
        zs = []
        for h in group:
            cols = slice(h * dh, (h + 1) * dh)
            q = q_ref[:, cols]
            z_d = lax.dot_general(q, kd_ref[:, cols], (((1,), (1,)), ((), ())),
                                  preferred_element_type=F32) * scale
            z_p = lax.dot_general(q, kp_ref[:, cols], (((1,), (1,)), ((), ())),
                                  preferred_element_type=F32) * scale
            zs += [jnp.where(causal, z_d, MASKED_SCORE), z_p]
        z = jnp.concatenate(zs, axis=0)
        used = jnp.maximum(z, 0.0) + jnp.log(1.0 + _exp(-jnp.abs(z)))
        logw = z - used - jnp.dot(used.astype(BF16), later, preferred_element_type=F32)
        used_rows = jnp.sum(used, axis=-1, keepdims=True)
        for n, h in enumerate(group):
            cols = slice(h * dh, (h + 1) * dh)
            rows_d = slice(2 * n * t, (2 * n + 1) * t)
            rows_p = slice((2 * n + 1) * t, (2 * n + 2) * t)
            used_d, used_p = used_rows[rows_d], used_rows[rows_p]
            w = jnp.concatenate([_exp(logw[rows_d]), _exp(logw[rows_p] - (used_d + no_prev))],
                                axis=1).astype(BF16)
            v = jnp.concatenate([vd_ref[:, cols], vp_ref[:, cols]], axis=0)
            acc_ref[:, cols] = jnp.dot(w, v, preferred_element_type=F32)
            used = used_d + used_p
            used_ref[h] = used
            least = used if least is None else jnp.minimum(least, used)
    low = jnp.min(least)

    def cond(state):
        kb, low = state
        return jnp.logical_and(kb >= 0, low <= STICK_USED_F32_ZERO)

    def body(state):
        kb, _ = state
        rows = pl.ds(pl.multiple_of(kb * t, t), t)
        k_copy = pltpu.make_async_copy(qkv_hbm.at[rows, pl.ds(width, width)], kbuf, sems.at[0])
        v_copy = pltpu.make_async_copy(qkv_hbm.at[rows, pl.ds(2 * width, width)], vbuf, sems.at[1])
        k_copy.start()
        v_copy.start()
        k_copy.wait()
        v_copy.wait()
        least = None
        for h in range(heads):
            cols = slice(h * dh, (h + 1) * dh)
            logw, used_b = block(h, kbuf, None)
            used = used_ref[h]
            acc_ref[:, cols] += pv(h, vbuf, _exp(logw - used))
            used = used + used_b
            used_ref[h] = used
            least = used if least is None else jnp.minimum(least, used)
        return kb - 1, jnp.min(least)

    lax.while_loop(cond, body, (qi - 2, low))
    o_ref[...] = acc_ref[...].astype(o_ref.dtype)


def _cast_every(rows, steps):
    every = 1
    while (rows * every) % (steps * BF16_SUBLANES) and every < steps:
        every *= 2
    return every


def _attention(qkv, weights):
    s = qkv.shape[0]
    t = ATT_BLOCK
    w = SB_HEADS * SB_HEAD_DIM
    steps = s // t
    prev = lambda i: jnp.maximum(i - 1, 0)
    cast_every = tuple(_cast_every(wt.shape[0], steps) for wt, _, _ in weights)
    cast_in_specs = [pl.BlockSpec((wt.shape[0] * e // steps, nc), lambda i, e=e, cb=cb: (i // e, cb))
                     for (wt, nc, cb), e in zip(weights, cast_every)]
    cast_out_specs = [pl.BlockSpec((wt.shape[0] * e // steps, nc), lambda i, e=e: (i // e, 0))
                      for (wt, nc, _), e in zip(weights, cast_every)]
    outs = pl.pallas_call(
        functools.partial(_attn_kernel, cast_every=cast_every),
        grid=(steps,),
        in_specs=[pl.BlockSpec((t, w), lambda i: (i, 0)),
                  pl.BlockSpec((t, w), lambda i: (i, 1)),
                  pl.BlockSpec((t, w), lambda i: (prev(i), 1)),
                  pl.BlockSpec((t, w), lambda i: (i, 2)),
                  pl.BlockSpec((t, w), lambda i: (prev(i), 2)),
                  pl.BlockSpec(memory_space=pl.ANY)] + cast_in_specs,
        out_specs=[pl.BlockSpec((t, w), lambda i: (i, 0))] + cast_out_specs,
        out_shape=[jax.ShapeDtypeStruct((s, w), BF16)]
        + [jax.ShapeDtypeStruct((wt.shape[0], nc), BF16) for wt, nc, _ in weights],
        scratch_shapes=[pltpu.VMEM((t, w), F32), pltpu.VMEM((SB_HEADS, t, 1), F32),
                        pltpu.VMEM((t, w), BF16), pltpu.VMEM((t, w), BF16),
                        pltpu.SemaphoreType.DMA((2,))],
        compiler_params=_params("arbitrary"),
        name="sb_attention",
    )(qkv, qkv, qkv, qkv, qkv, qkv, *[wt for wt, _, _ in weights])
    return outs[0], outs[1:]


def _ffn_kernel(hn_ref, x_hbm, wg_ref, wu_ref, wd_ref, gf_ref, o_ref, x_scr, x_sem, *, final_norm):
    i = pl.program_id(0)
    f = pl.program_id(1)
    tm = o_ref.shape[0]
    x_copy = pltpu.make_async_copy(x_hbm.at[pl.ds(pl.multiple_of(i * tm, tm), tm), :], x_scr, x_sem)

    @pl.when(f == 0)
    def _():
        x_copy.start()
        o_ref[...] = jnp.zeros_like(o_ref)

    hn = hn_ref[...]
    gate = jnp.dot(hn, wg_ref[...], preferred_element_type=F32)
    up = jnp.dot(hn, wu_ref[...], preferred_element_type=F32)
    h = (gate * jax.nn.sigmoid(gate) * up).astype(BF16)
    tn = _pick(o_ref.shape[1], wg_ref.shape[1])
    for c in range(o_ref.shape[1] // tn):
        cols = slice(c * tn, (c + 1) * tn)
        o_ref[:, cols] += jnp.dot(h, wd_ref[:, cols], preferred_element_type=F32)

    @pl.when(f == pl.num_programs(1) - 1)
    def _():
        x_copy.wait()
        for r in range(0, tm, FINAL_ROWS):
            rows = slice(r, r + FINAL_ROWS)
            y = x_scr[rows, :] + o_ref[rows, :]
            if final_norm:
                ms = jnp.mean(y * y, axis=-1, keepdims=True)
                y = y * lax.rsqrt(ms + EPS) * gf_ref[...]
            o_ref[rows, :] = y


def _ffn(hn, x, w_gate_up, w_down, g_final, tm, tf, final_norm):
    s, d = x.shape
    d_ff = w_down.shape[0]
    nf = d_ff // tf
    return pl.pallas_call(
        functools.partial(_ffn_kernel, final_norm=final_norm),
        grid=(s // tm, nf),
        in_specs=[pl.BlockSpec((tm, d), lambda i, f: (i, 0)),
                  pl.BlockSpec(memory_space=pl.ANY),
                  pl.BlockSpec((d, tf), lambda i, f: (0, f)),
                  pl.BlockSpec((d, tf), lambda i, f: (0, nf + f)),
                  pl.BlockSpec((tf, d), lambda i, f: (f, 0)),
                  pl.BlockSpec((1, d), lambda i, f: (0, 0))],
        out_specs=pl.BlockSpec((tm, d), lambda i, f: (i, 0)),
        out_shape=jax.ShapeDtypeStruct((s, d), F32),
        scratch_shapes=[pltpu.VMEM((tm, d), F32), pltpu.SemaphoreType.DMA(())],
        compiler_params=_params("arbitrary", "arbitrary"),
        name="ffn",
    )(hn, x, w_gate_up, w_gate_up, w_down, g_final.reshape(1, d))


def _gated_merge_kernel(xn_ref, a_ref, b_ref, x_ref, *refs):
    o_ref, hn_ref = refs[-2:]
    bg_ref, wa_ref, wb_ref, wo_ref, gn_ref = refs[-7:-2]
    wg_refs = refs[:-7]
    nc = len(wg_refs) // 2
    d = x_ref.shape[1]
    tn = d // nc
    xn = xn_ref[...]
    a = a_ref[...]
    b = b_ref[...]
    parts = []
    for c in range(nc):
        cols = slice(c * tn, (c + 1) * tn)
        g_a = jax.nn.sigmoid(jnp.dot(xn, wg_refs[c][...], preferred_element_type=F32)
                             + bg_ref[:, cols])
        g_b = jax.nn.sigmoid(jnp.dot(xn, wg_refs[nc + c][...], preferred_element_type=F32)
                             + bg_ref[:, d + c * tn:d + (c + 1) * tn])
        y_a = jnp.dot(a, wa_ref[:, cols], preferred_element_type=F32)
        y_b = jnp.dot(b, wb_ref[:, cols], preferred_element_type=F32)
        parts.append((g_a * y_a + g_b * y_b).astype(BF16))
    m = jnp.concatenate(parts, axis=1)
    y = x_ref[...] + jnp.dot(m, wo_ref[...], preferred_element_type=F32)
    o_ref[...] = y
    ms = jnp.mean(y * y, axis=-1, keepdims=True)
    hn_ref[...] = (y * lax.rsqrt(ms + EPS) * gn_ref[...]).astype(BF16)


def _gated_merge(xn, a, b, x, gate_tiles, b_gate, w_a, w_b, w_o, g_norm, tm):
    s, d = x.shape
    wa = a.shape[1]
    wb = b.shape[1]
    resident = dict(pipeline_mode=pl.Buffered(1))
    gate_specs = [pl.BlockSpec(g.shape, lambda i: (0, 0), **resident) for g in gate_tiles]
    return pl.pallas_call(
        _gated_merge_kernel,
        grid=(s // tm,),
        in_specs=[pl.BlockSpec((tm, d), lambda i: (i, 0)),
                  pl.BlockSpec((tm, wa), lambda i: (i, 0)),
                  pl.BlockSpec((tm, wb), lambda i: (i, 0)),
                  pl.BlockSpec((tm, d), lambda i: (i, 0))]
        + gate_specs
        + [pl.BlockSpec((1, 2 * d), lambda i: (0, 0)),
           pl.BlockSpec((wa, d), lambda i: (0, 0), **resident),
           pl.BlockSpec((wb, d), lambda i: (0, 0), **resident),
           pl.BlockSpec((d, d), lambda i: (0, 0), **resident),
           pl.BlockSpec((1, d), lambda i: (0, 0))],
        out_specs=[pl.BlockSpec((tm, d), lambda i: (i, 0)),
                   pl.BlockSpec((tm, d), lambda i: (i, 0))],
        out_shape=[jax.ShapeDtypeStruct((s, d), F32),
                   jax.ShapeDtypeStruct((s, d), BF16)],
        compiler_params=_params("arbitrary"),
        name="gated_merge",
    )(xn, a, b, x, *gate_tiles, b_gate.reshape(1, 2 * d), w_a, w_b, w_o, g_norm.reshape(1, d))


def _pick(n, pref):
    return pref if n % pref == 0 else n


def kernel(x, norm_mix, w_in, b_gate, ln_v_g, ln_v_b, w_s, b_s, w_proj_a, w_proj_b,
           w_out, norm_ffn, w_gate_up, w_down, norm_final):
    batch, seq, d = x.shape
    depth = w_in.shape[0]
    sgu_w = SGU_GROUPS * SGU_GROUP_DIM
    sb_w = SB_HEADS * SB_HEAD_DIM
    assert sgu_w == sb_w
    off_q = 2 * sgu_w
    off_g = off_q + 3 * sb_w
    outs = []
    for b in range(batch):
        h = x.reshape(seq, d) if batch == 1 else x[b]
        for l in range(depth):
            w_first = w_in[l][:, :off_g].astype(BF16)
            xn, a, qkv = _sgu(h, norm_mix[l], w_first, ln_v_g[l], ln_v_b[l], w_s[l], b_s[l],
                              _pick(seq, 512))
            tn = _pick(d, 1024)
            whole = lambda wt: (wt, wt.shape[1], 0)
            gate_cols = [(w_in[l], tn, off_g // tn + k) for k in range(2 * d // tn)]
            o_b, bf = _attention(qkv, [whole(w_proj_a[l]), whole(w_proj_b[l]), whole(w_out[l]),
                                       whole(w_gate_up[l]), whole(w_down[l])] + gate_cols)
            wa_l, wb_l, wo_l, wgu_l, wd_l = bf[:5]
            h, hn = _gated_merge(xn, a, o_b, h, bf[5:], b_gate[l], wa_l, wb_l, wo_l,
                                 norm_ffn[l], _pick(seq, 256))
            h = _ffn(hn, h, wgu_l, wd_l, norm_final, _pick(seq, 1024),
                     _pick(w_down.shape[1], 512), final_norm=(l == depth - 1))
        outs.append(h)
    return outs[0].reshape(x.shape) if batch == 1 else jnp.stack(outs)
```

```python
import functools
import math

import jax
import jax.numpy as jnp
from jax import lax
from jax.experimental import pallas as pl
from jax.experimental.pallas import tpu as pltpu

EPS = 1e-6
CHUNK = 128
SGU_GROUPS = 8
SGU_GROUP_DIM = 128
SB_HEADS = 8
SB_HEAD_DIM = 128
ATT_BLOCK = 256
BF16_SUBLANES = 16
FINAL_ROWS = 16
STICK_USED_F32_ZERO = 150.0
MASKED_SCORE = -1e30
SOFTPLUS_LINEAR_ABOVE = 90.0
LOG2_E = math.log2(math.e)
ATT_HEADS_PER_TRIANGLE = 2
VMEM_LIMIT_BYTES = 60 * 1024 * 1024

F32 = jnp.float32
BF16 = jnp.bfloat16


def _params(*sem):
    return pltpu.CompilerParams(dimension_semantics=sem, vmem_limit_bytes=VMEM_LIMIT_BYTES)


def _softplus_bits(z):
    return jnp.where(z > SOFTPLUS_LINEAR_ABOVE, z, jnp.log(1.0 + jnp.exp2(z)) * LOG2_E)


def _gelu(x):
    return 0.5 * x * (1.0 + lax.erf(x * math.sqrt(0.5)))


def _sgu_kernel(x_ref, gn_ref, w_ref, wq_ref, wk_ref, wv_ref, lng_ref, lnb_ref, ws_ref, bs_ref,
                xn_ref, o_ref, qkv_ref):
    width = o_ref.shape[1]
    x = x_ref[...]
    ms = jnp.mean(x * x, axis=-1, keepdims=True)
    xn = (x * lax.rsqrt(ms + EPS) * gn_ref[...]).astype(BF16)
    xn_ref[...] = xn
    v = _gelu(jnp.dot(xn, w_ref[:, width:], preferred_element_type=F32))
    qkv_ref[:, :width] = jnp.dot(xn, wq_ref[...], preferred_element_type=F32).astype(BF16)
    mu = jnp.mean(v, axis=-1, keepdims=True)
    cen = v - mu
    var = jnp.mean(cen * cen, axis=-1, keepdims=True)
    vln = (cen * lax.rsqrt(var + EPS) * lng_ref[...] + lnb_ref[...]).astype(BF16)
    u = _gelu(jnp.dot(xn, w_ref[:, :width], preferred_element_type=F32))
    qkv_ref[:, width:2 * width] = jnp.dot(xn, wk_ref[...], preferred_element_type=F32).astype(BF16)
    qkv_ref[:, 2 * width:] = jnp.dot(xn, wv_ref[...], preferred_element_type=F32).astype(BF16)
    t_idx = lax.broadcasted_iota(jnp.int32, (CHUNK, CHUNK), 0)
    s_idx = lax.broadcasted_iota(jnp.int32, (CHUNK, CHUNK), 1)
    causal = s_idx <= t_idx
    for g in range(SGU_GROUPS):
        wm = jnp.where(causal, ws_ref[g], 0.0).astype(BF16)
        cols = slice(g * SGU_GROUP_DIM, (g + 1) * SGU_GROUP_DIM)
        for c in range(o_ref.shape[0] // CHUNK):
            rows = slice(c * CHUNK, (c + 1) * CHUNK)
            mixed = jnp.dot(wm, vln[rows, cols], preferred_element_type=F32) + bs_ref[g]
            o_ref[rows, cols] = (u[rows, cols] * mixed).astype(o_ref.dtype)


def _sgu(x, g_norm, w_in, ln_g, ln_b, w_s, b_s, tm):
    s, d = x.shape
    width = SGU_GROUPS * SGU_GROUP_DIM
    resident = dict(pipeline_mode=pl.Buffered(1))
    return pl.pallas_call(
        _sgu_kernel,
        grid=(s // tm,),
        in_specs=[pl.BlockSpec((tm, d), lambda i: (i, 0)),
                  pl.BlockSpec((1, d), lambda i: (0, 0)),
                  pl.BlockSpec((d, 2 * width), lambda i: (0, 0), **resident),
                  pl.BlockSpec((d, width), lambda i: (0, 2), **resident),
                  pl.BlockSpec((d, width), lambda i: (0, 3), **resident),
                  pl.BlockSpec((d, width), lambda i: (0, 4), **resident),
                  pl.BlockSpec((1, width), lambda i: (0, 0)),
                  pl.BlockSpec((1, width), lambda i: (0, 0)),
                  pl.BlockSpec(w_s.shape, lambda i: (0, 0, 0)),
                  pl.BlockSpec((SGU_GROUPS, CHUNK, 1), lambda i: (0, 0, 0))],
        out_specs=[pl.BlockSpec((tm, d), lambda i: (i, 0)),
                   pl.BlockSpec((tm, width), lambda i: (i, 0)),
                   pl.BlockSpec((tm, 3 * width), lambda i: (i, 0))],
        out_shape=[jax.ShapeDtypeStruct((s, d), BF16),
                   jax.ShapeDtypeStruct((s, width), BF16),
                   jax.ShapeDtypeStruct((s, 3 * width), BF16)],
        compiler_params=_params("arbitrary"),
        name="sgu_qkv",
    )(x, g_norm.reshape(1, d), w_in, w_in, w_in, w_in, ln_g.reshape(1, width),
      ln_b.reshape(1, width), w_s, b_s.reshape(SGU_GROUPS, CHUNK, 1))


def _attn_kernel(*refs, cast_every):
    n_cast = len(cast_every)
    q_ref, kd_ref, kp_ref, vd_ref, vp_ref, qkv_hbm = refs[:6]
    cast_in = refs[6:6 + n_cast]
    o_ref = refs[6 + n_cast]
    cast_out = refs[7 + n_cast:7 + 2 * n_cast]
    acc_ref, used_ref, kbuf, vbuf, sems = refs[7 + 2 * n_cast:]
    t = q_ref.shape[0]
    dh = SB_HEAD_DIM
    width = q_ref.shape[1]
    heads = width // dh
    qi = pl.program_id(0)
    scale = LOG2_E / math.sqrt(dh)
    row = lax.broadcasted_iota(jnp.int32, (t, t), 0)
    col = lax.broadcasted_iota(jnp.int32, (t, t), 1)
    later = (row > col).astype(BF16)
    causal = col < row
    no_prev = jnp.where(qi > 0, 0.0, -MASKED_SCORE)

    for src, dst, every in zip(cast_in, cast_out, cast_every):
        if every == 1:
            dst[...] = src[...].astype(BF16)
        else:
            @pl.when(qi % every == 0)
            def _(src=src, dst=dst):
                dst[...] = src[...].astype(BF16)

    def block(h, k_ref, mask):
        cols = slice(h * dh, (h + 1) * dh)
        z = lax.dot_general(q_ref[:, cols], k_ref[:, cols], (((1,), (1,)), ((), ())),
                            preferred_element_type=F32) * scale
        if mask is not None:
            z = jnp.where(mask, z, MASKED_SCORE)
        used = _softplus_bits(z)
        log_beta = z - used
        after = jnp.dot(used.astype(BF16), later, preferred_element_type=F32)
        return log_beta - after, jnp.sum(used, axis=-1, keepdims=True)

    def pv(h, v_ref, w):
        cols = slice(h * dh, (h + 1) * dh)
        return jnp.dot(w.astype(BF16), v_ref[:, cols], preferred_element_type=F32)

    least = None
    for h0 in range(0, heads, ATT_HEADS_PER_TRIANGLE):
        group = range(h0, h0 + ATT_HEADS_PER_TRIANGLE)
        zs = []
        for h in group:
            cols = slice(h * dh, (h + 1) * dh)
            q = q_ref[:, cols]
            z_d = lax.dot_general(q, kd_ref[:, cols], (((1,), (1,)), ((), ())),
                                  preferred_element_type=F32) * scale
            z_p = lax.dot_general(q, kp_ref[:, cols], (((1,), (1,)), ((), ())),
                                  preferred_element_type=F32) * scale
            zs += [jnp.where(causal, z_d, MASKED_SCORE), z_p]
        z = jnp.concatenate(zs, axis=0)
        used = _softplus_bits(z)
        logw = z - used - jnp.dot(used.astype(BF16), later, preferred_element_type=F32)
        used_rows = jnp.sum(used, axis=-1, keepdims=True)
        for n, h in enumerate(group):
            cols = slice(h * dh, (h + 1) * dh)
            rows_d = slice(2 * n * t, (2 * n + 1) * t)
            rows_p = slice((2 * n + 1) * t, (2 * n + 2) * t)
            used_d, used_p = used_rows[rows_d], used_rows[rows_p]
            w = jnp.concatenate([jnp.exp2(logw[rows_d]),
                                 jnp.exp2(logw[rows_p] - (used_d + no_prev))], axis=1).astype(BF16)
            v = jnp.concatenate([vd_ref[:, cols], vp_ref[:, cols]], axis=0)
            acc_ref[:, cols] = jnp.dot(w, v, preferred_element_type=F32)
            used = used_d + used_p
            used_ref[h] = used
            least = used if least is None else jnp.minimum(least, used)
    low = jnp.min(least)

    def cond(state):
        kb, low = state
        return jnp.logical_and(kb >= 0, low <= STICK_USED_F32_ZERO)

    def body(state):
        kb, _ = state
        rows = pl.ds(pl.multiple_of(kb * t, t), t)
        k_copy = pltpu.make_async_copy(qkv_hbm.at[rows, pl.ds(width, width)], kbuf, sems.at[0])
        v_copy = pltpu.make_async_copy(qkv_hbm.at[rows, pl.ds(2 * width, width)], vbuf, sems.at[1])
        k_copy.start()
        v_copy.start()
        k_copy.wait()
        v_copy.wait()
        least = None
        for h in range(heads):
            cols = slice(h * dh, (h + 1) * dh)
            logw, used_b = block(h, kbuf, None)
            used = used_ref[h]
            acc_ref[:, cols] += pv(h, vbuf, jnp.exp2(logw - used))
            used = used + used_b
            used_ref[h] = used
            least = used if least is None else jnp.minimum(least, used)
        return kb - 1, jnp.min(least)

    lax.while_loop(cond, body, (qi - 2, low))
    o_ref[...] = acc_ref[...].astype(o_ref.dtype)


def _cast_every(rows, steps):
    every = 1
    while (rows * every) % (steps * BF16_SUBLANES) and every < steps:
        every *= 2
    return every


def _attention(qkv, weights):
    s = qkv.shape[0]
    t = ATT_BLOCK
    w = SB_HEADS * SB_HEAD_DIM
    steps = s // t
    prev = lambda i: jnp.maximum(i - 1, 0)
    cast_every = tuple(_cast_every(wt.shape[0], steps) for wt, _, _ in weights)
    cast_in_specs = [pl.BlockSpec((wt.shape[0] * e // steps, nc), lambda i, e=e, cb=cb: (i // e, cb))
                     for (wt, nc, cb), e in zip(weights, cast_every)]
    cast_out_specs = [pl.BlockSpec((wt.shape[0] * e // steps, nc), lambda i, e=e: (i // e, 0))
                      for (wt, nc, _), e in zip(weights, cast_every)]
    outs = pl.pallas_call(
        functools.partial(_attn_kernel, cast_every=cast_every),
        grid=(steps,),
        in_specs=[pl.BlockSpec((t, w), lambda i: (i, 0)),
                  pl.BlockSpec((t, w), lambda i: (i, 1)),
                  pl.BlockSpec((t, w), lambda i: (prev(i), 1)),
                  pl.BlockSpec((t, w), lambda i: (i, 2)),
                  pl.BlockSpec((t, w), lambda i: (prev(i), 2)),
                  pl.BlockSpec(memory_space=pl.ANY)] + cast_in_specs,
        out_specs=[pl.BlockSpec((t, w), lambda i: (i, 0))] + cast_out_specs,
        out_shape=[jax.ShapeDtypeStruct((s, w), BF16)]
        + [jax.ShapeDtypeStruct((wt.shape[0], nc), BF16) for wt, nc, _ in weights],
        scratch_shapes=[pltpu.VMEM((t, w), F32), pltpu.VMEM((SB_HEADS, t, 1), F32),
                        pltpu.VMEM((t, w), BF16), pltpu.VMEM((t, w), BF16),
                        pltpu.SemaphoreType.DMA((2,))],
        compiler_params=_params("arbitrary"),
        name="sb_attention",
    )(qkv, qkv, qkv, qkv, qkv, qkv, *[wt for wt, _, _ in weights])
    return outs[0], outs[1:]


def _ffn_kernel(hn_ref, x_hbm, wg_ref, wu_ref, wd_ref, gf_ref, o_ref, x_scr, x_sem, *, final_norm):
    i = pl.program_id(0)
    f = pl.program_id(1)
    tm = o_ref.shape[0]
    x_copy = pltpu.make_async_copy(x_hbm.at[pl.ds(pl.multiple_of(i * tm, tm), tm), :], x_scr, x_sem)

    @pl.when(f == 0)
    def _():
        x_copy.start()
        o_ref[...] = jnp.zeros_like(o_ref)

    hn = hn_ref[...]
    gate = jnp.dot(hn, wg_ref[...], preferred_element_type=F32)
    up = jnp.dot(hn, wu_ref[...], preferred_element_type=F32)
    h = (gate * jax.nn.sigmoid(gate) * up).astype(BF16)
    tn = _pick(o_ref.shape[1], wg_ref.shape[1])
    for c in range(o_ref.shape[1] // tn):
        cols = slice(c * tn, (c + 1) * tn)
        o_ref[:, cols] += jnp.dot(h, wd_ref[:, cols], preferred_element_type=F32)

    @pl.when(f == pl.num_programs(1) - 1)
    def _():
        x_copy.wait()
        for r in range(0, tm, FINAL_ROWS):
            rows = slice(r, r + FINAL_ROWS)
            y = x_scr[rows, :] + o_ref[rows, :]
            if final_norm:
                ms = jnp.mean(y * y, axis=-1, keepdims=True)
                y = y * lax.rsqrt(ms + EPS) * gf_ref[...]
            o_ref[rows, :] = y


def _ffn(hn, x, w_gate_up, w_down, g_final, tm, tf, final_norm):
    s, d = x.shape
    d_ff = w_down.shape[0]
    nf = d_ff // tf
    return pl.pallas_call(
        functools.partial(_ffn_kernel, final_norm=final_norm),
        grid=(s // tm, nf),
        in_specs=[pl.BlockSpec((tm, d), lambda i, f: (i, 0)),
                  pl.BlockSpec(memory_space=pl.ANY),
                  pl.BlockSpec((d, tf), lambda i, f: (0, f)),
                  pl.BlockSpec((d, tf), lambda i, f: (0, nf + f)),
                  pl.BlockSpec((tf, d), lambda i, f: (f, 0)),
                  pl.BlockSpec((1, d), lambda i, f: (0, 0))],
        out_specs=pl.BlockSpec((tm, d), lambda i, f: (i, 0)),
        out_shape=jax.ShapeDtypeStruct((s, d), F32),
        scratch_shapes=[pltpu.VMEM((tm, d), F32), pltpu.SemaphoreType.DMA(())],
        compiler_params=_params("arbitrary", "arbitrary"),
        name="ffn",
    )(hn, x, w_gate_up, w_gate_up, w_down, g_final.reshape(1, d))


def _gated_merge_kernel(xn_ref, a_ref, b_ref, x_ref, *refs):
    o_ref, hn_ref = refs[-2:]
    bg_ref, wa_ref, wb_ref, wo_ref, gn_ref = refs[-7:-2]
    wg_refs = refs[:-7]
    nc = len(wg_refs) // 2
    d = x_ref.shape[1]
    tn = d // nc
    xn = xn_ref[...]
    a = a_ref[...]
    b = b_ref[...]
    parts = []
    for c in range(nc):
        cols = slice(c * tn, (c + 1) * tn)
        g_a = jax.nn.sigmoid(jnp.dot(xn, wg_refs[c][...], preferred_element_type=F32)
                             + bg_ref[:, cols])
        g_b = jax.nn.sigmoid(jnp.dot(xn, wg_refs[nc + c][...], preferred_element_type=F32)
                             + bg_ref[:, d + c * tn:d + (c + 1) * tn])
        y_a = jnp.dot(a, wa_ref[:, cols], preferred_element_type=F32)
        y_b = jnp.dot(b, wb_ref[:, cols], preferred_element_type=F32)
        parts.append((g_a * y_a + g_b * y_b).astype(BF16))
    m = jnp.concatenate(parts, axis=1)
    y = x_ref[...] + jnp.dot(m, wo_ref[...], preferred_element_type=F32)
    o_ref[...] = y
    ms = jnp.mean(y * y, axis=-1, keepdims=True)
    hn_ref[...] = (y * lax.rsqrt(ms + EPS) * gn_ref[...]).astype(BF16)


def _gated_merge(xn, a, b, x, gate_tiles, b_gate, w_a, w_b, w_o, g_norm, tm):
    s, d = x.shape
    wa = a.shape[1]
    wb = b.shape[1]
    resident = dict(pipeline_mode=pl.Buffered(1))
    gate_specs = [pl.BlockSpec(g.shape, lambda i: (0, 0), **resident) for g in gate_tiles]
    return pl.pallas_call(
        _gated_merge_kernel,
        grid=(s // tm,),
        in_specs=[pl.BlockSpec((tm, d), lambda i: (i, 0)),
                  pl.BlockSpec((tm, wa), lambda i: (i, 0)),
                  pl.BlockSpec((tm, wb), lambda i: (i, 0)),
                  pl.BlockSpec((tm, d), lambda i: (i, 0))]
        + gate_specs
        + [pl.BlockSpec((1, 2 * d), lambda i: (0, 0)),
           pl.BlockSpec((wa, d), lambda i: (0, 0), **resident),
           pl.BlockSpec((wb, d), lambda i: (0, 0), **resident),
           pl.BlockSpec((d, d), lambda i: (0, 0), **resident),
           pl.BlockSpec((1, d), lambda i: (0, 0))],
        out_specs=[pl.BlockSpec((tm, d), lambda i: (i, 0)),
                   pl.BlockSpec((tm, d), lambda i: (i, 0))],
        out_shape=[jax.ShapeDtypeStruct((s, d), F32),
                   jax.ShapeDtypeStruct((s, d), BF16)],
        compiler_params=_params("arbitrary"),
        name="gated_merge",
    )(xn, a, b, x, *gate_tiles, b_gate.reshape(1, 2 * d), w_a, w_b, w_o, g_norm.reshape(1, d))


def _pick(n, pref):
    return pref if n % pref == 0 else n


def kernel(x, norm_mix, w_in, b_gate, ln_v_g, ln_v_b, w_s, b_s, w_proj_a, w_proj_b,
           w_out, norm_ffn, w_gate_up, w_down, norm_final):
    batch, seq, d = x.shape
    depth = w_in.shape[0]
    sgu_w = SGU_GROUPS * SGU_GROUP_DIM
    sb_w = SB_HEADS * SB_HEAD_DIM
    assert sgu_w == sb_w
    off_q = 2 * sgu_w
    off_g = off_q + 3 * sb_w
    outs = []
    for b in range(batch):
        h = x.reshape(seq, d) if batch == 1 else x[b]
        for l in range(depth):
            w_first = w_in[l][:, :off_g].astype(BF16)
            xn, a, qkv = _sgu(h, norm_mix[l], w_first, ln_v_g[l], ln_v_b[l], w_s[l], b_s[l],
                              _pick(seq, 512))
            tn = _pick(d, 1024)
            whole = lambda wt: (wt, wt.shape[1], 0)
            gate_cols = [(w_in[l], tn, off_g // tn + k) for k in range(2 * d // tn)]
            o_b, bf = _attention(qkv, [whole(w_proj_a[l]), whole(w_proj_b[l]), whole(w_out[l]),
                                       whole(w_gate_up[l]), whole(w_down[l])] + gate_cols)
            wa_l, wb_l, wo_l, wgu_l, wd_l = bf[:5]
            h, hn = _gated_merge(xn, a, o_b, h, bf[5:], b_gate[l], wa_l, wb_l, wo_l,
                                 norm_ffn[l], _pick(seq, 256))
            h = _ffn(hn, h, wgu_l, wd_l, norm_final, _pick(seq, 1024),
                     _pick(w_down.shape[1], 512), final_norm=(l == depth - 1))
        outs.append(h)
    return outs[0].reshape(x.shape) if batch == 1 else jnp.stack(outs)
```

```python
import functools
import math

import jax
import jax.numpy as jnp
from jax import lax
from jax.experimental import pallas as pl
from jax.experimental.pallas import tpu as pltpu

EPS = 1e-6
CHUNK = 128
SGU_GROUPS = 8
SGU_GROUP_DIM = 128
SB_HEADS = 8
SB_HEAD_DIM = 128
ATT_BLOCK = 256
BF16_SUBLANES = 16
FINAL_ROWS = 16
STICK_USED_F32_ZERO = 150.0
MASKED_SCORE = -1e30
SOFTPLUS_LINEAR_ABOVE = 90.0
LOG2_E = math.log2(math.e)
ATT_HEADS_PER_TRIANGLE = 2
VMEM_LIMIT_BYTES = 60 * 1024 * 1024

F32 = jnp.float32
BF16 = jnp.bfloat16


def _params(*sem):
    return pltpu.CompilerParams(dimension_semantics=sem, vmem_limit_bytes=VMEM_LIMIT_BYTES)


def _softplus_bits(z):
    return jnp.where(z > SOFTPLUS_LINEAR_ABOVE, z, jnp.log(1.0 + jnp.exp2(z)) * LOG2_E)


def _gelu(x):
    return 0.5 * x * (1.0 + lax.erf(x * math.sqrt(0.5)))


def _sgu_kernel(x_ref, gn_ref, w_ref, wq_ref, wk_ref, wv_ref, lng_ref, lnb_ref, ws_ref, bs_ref,
                xn_ref, o_ref, qkv_ref):
    width = o_ref.shape[1]
    x = x_ref[...]
    ms = jnp.mean(x * x, axis=-1, keepdims=True)
    xn = (x * lax.rsqrt(ms + EPS) * gn_ref[...]).astype(BF16)
    xn_ref[...] = xn
    v = _gelu(jnp.dot(xn, w_ref[:, width:], preferred_element_type=F32))
    qkv_ref[:, :width] = jnp.dot(xn, wq_ref[...], preferred_element_type=F32).astype(BF16)
    mu = jnp.mean(v, axis=-1, keepdims=True)
    cen = v - mu
    var = jnp.mean(cen * cen, axis=-1, keepdims=True)
    vln = (cen * lax.rsqrt(var + EPS) * lng_ref[...] + lnb_ref[...]).astype(BF16)
    u = _gelu(jnp.dot(xn, w_ref[:, :width], preferred_element_type=F32))
    qkv_ref[:, width:2 * width] = jnp.dot(xn, wk_ref[...], preferred_element_type=F32).astype(BF16)
    qkv_ref[:, 2 * width:] = jnp.dot(xn, wv_ref[...], preferred_element_type=F32).astype(BF16)
    t_idx = lax.broadcasted_iota(jnp.int32, (CHUNK, CHUNK), 0)
    s_idx = lax.broadcasted_iota(jnp.int32, (CHUNK, CHUNK), 1)
    causal = s_idx <= t_idx
    for g in range(SGU_GROUPS):
        wm = jnp.where(causal, ws_ref[g], 0.0).astype(BF16)
        cols = slice(g * SGU_GROUP_DIM, (g + 1) * SGU_GROUP_DIM)
        for c in range(o_ref.shape[0] // CHUNK):
            rows = slice(c * CHUNK, (c + 1) * CHUNK)
            mixed = jnp.dot(wm, vln[rows, cols], preferred_element_type=F32) + bs_ref[g]
            o_ref[rows, cols] = (u[rows, cols] * mixed).astype(o_ref.dtype)


def _sgu(x, g_norm, w_in, ln_g, ln_b, w_s, b_s, tm):
    s, d = x.shape
    width = SGU_GROUPS * SGU_GROUP_DIM
    resident = dict(pipeline_mode=pl.Buffered(1))
    return pl.pallas_call(
        _sgu_kernel,
        grid=(s // tm,),
        in_specs=[pl.BlockSpec((tm, d), lambda i: (i, 0)),
                  pl.BlockSpec((1, d), lambda i: (0, 0)),
                  pl.BlockSpec((d, 2 * width), lambda i: (0, 0), **resident),
                  pl.BlockSpec((d, width), lambda i: (0, 2), **resident),
                  pl.BlockSpec((d, width), lambda i: (0, 3), **resident),
                  pl.BlockSpec((d, width), lambda i: (0, 4), **resident),
                  pl.BlockSpec((1, width), lambda i: (0, 0)),
                  pl.BlockSpec((1, width), lambda i: (0, 0)),
                  pl.BlockSpec(w_s.shape, lambda i: (0, 0, 0)),
                  pl.BlockSpec((SGU_GROUPS, CHUNK, 1), lambda i: (0, 0, 0))],
        out_specs=[pl.BlockSpec((tm, d), lambda i: (i, 0)),
                   pl.BlockSpec((tm, width), lambda i: (i, 0)),
                   pl.BlockSpec((tm, 3 * width), lambda i: (i, 0))],
        out_shape=[jax.ShapeDtypeStruct((s, d), BF16),
                   jax.ShapeDtypeStruct((s, width), BF16),
                   jax.ShapeDtypeStruct((s, 3 * width), BF16)],
        compiler_params=_params("arbitrary"),
        name="sgu_qkv",
    )(x, g_norm.reshape(1, d), w_in, w_in, w_in, w_in, ln_g.reshape(1, width),
      ln_b.reshape(1, width), w_s, b_s.reshape(SGU_GROUPS, CHUNK, 1))


def _attn_kernel(*refs, cast_every):
    n_cast = len(cast_every)
    q_ref, kd_ref, kp_ref, vd_ref, vp_ref, qkv_hbm = refs[:6]
    cast_in = refs[6:6 + n_cast]
    o_ref = refs[6 + n_cast]
    cast_out = refs[7 + n_cast:7 + 2 * n_cast]
    acc_ref, used_ref, kbuf, vbuf, sems = refs[7 + 2 * n_cast:]
    t = q_ref.shape[0]
    dh = SB_HEAD_DIM
    width = q_ref.shape[1]
    heads = width // dh
    qi = pl.program_id(0)
    scale = LOG2_E / math.sqrt(dh)
    row = lax.broadcasted_iota(jnp.int32, (t, t), 0)
    col = lax.broadcasted_iota(jnp.int32, (t, t), 1)
    later = (row > col).astype(BF16)
    causal = col < row
    no_prev = jnp.where(qi > 0, 0.0, -MASKED_SCORE)

    for src, dst, every in zip(cast_in, cast_out, cast_every):
        if every == 1:
            dst[...] = src[...].astype(BF16)
        else:
            @pl.when(qi % every == 0)
            def _(src=src, dst=dst):
                dst[...] = src[...].astype(BF16)

    def block(h, k_ref):
        cols = slice(h * dh, (h + 1) * dh)
        z = lax.dot_general(q_ref[:, cols], k_ref[:, cols], (((1,), (1,)), ((), ())),
                            preferred_element_type=F32) * scale
        used = _softplus_bits(z)
        log_beta = z - used
        after = jnp.dot(used.astype(BF16), later, preferred_element_type=F32)
        return log_beta - after, jnp.sum(used, axis=-1, keepdims=True)

    def pv(h, v_ref, w):
        cols = slice(h * dh, (h + 1) * dh)
        return jnp.dot(w.astype(BF16), v_ref[:, cols], preferred_element_type=F32)

    least = None
    for h0 in range(0, heads, ATT_HEADS_PER_TRIANGLE):
        group = range(h0, h0 + ATT_HEADS_PER_TRIANGLE)
        zs = []
        for h in group:
            cols = slice(h * dh, (h + 1) * dh)
            q = q_ref[:, cols]
            z_d = lax.dot_general(q, kd_ref[:, cols], (((1,), (1,)), ((), ())),
                                  preferred_element_type=F32) * scale
            z_p = lax.dot_general(q, kp_ref[:, cols], (((1,), (1,)), ((), ())),
                                  preferred_element_type=F32) * scale
            zs += [jnp.where(causal, z_d, MASKED_SCORE), z_p]
        z = jnp.concatenate(zs, axis=0)
        used = _softplus_bits(z)
        logw = z - used - jnp.dot(used.astype(BF16), later, preferred_element_type=F32)
        used_rows = jnp.sum(used, axis=-1, keepdims=True)
        for n, h in enumerate(group):
            cols = slice(h * dh, (h + 1) * dh)
            rows_d = slice(2 * n * t, (2 * n + 1) * t)
            rows_p = slice((2 * n + 1) * t, (2 * n + 2) * t)
            used_d, used_p = used_rows[rows_d], used_rows[rows_p]
            w = jnp.concatenate([jnp.exp2(logw[rows_d]),
                                 jnp.exp2(logw[rows_p] - (used_d + no_prev))], axis=1).astype(BF16)
            v = jnp.concatenate([vd_ref[:, cols], vp_ref[:, cols]], axis=0)
            acc_ref[:, cols] = jnp.dot(w, v, preferred_element_type=F32)
            used = used_d + used_p
            used_ref[h] = used
            least = used if least is None else jnp.minimum(least, used)
    low = jnp.min(least)

    def cond(state):
        kb, low = state
        return jnp.logical_and(kb >= 0, low <= STICK_USED_F32_ZERO)

    def body(state):
        kb, _ = state
        rows = pl.ds(pl.multiple_of(kb * t, t), t)
        k_copy = pltpu.make_async_copy(qkv_hbm.at[rows, pl.ds(width, width)], kbuf, sems.at[0])
        v_copy = pltpu.make_async_copy(qkv_hbm.at[rows, pl.ds(2 * width, width)], vbuf, sems.at[1])
        k_copy.start()
        v_copy.start()
        k_copy.wait()
        v_copy.wait()
        least = None
        for h in range(heads):
            cols = slice(h * dh, (h + 1) * dh)
            logw, used_b = block(h, kbuf)
            used = used_ref[h]
            acc_ref[:, cols] += pv(h, vbuf, jnp.exp2(logw - used))
            used = used + used_b
            used_ref[h] = used
            least = used if least is None else jnp.minimum(least, used)
        return kb - 1, jnp.min(least)

    lax.while_loop(cond, body, (qi - 2, low))
    o_ref[...] = acc_ref[...].astype(o_ref.dtype)


def _cast_every(rows, steps):
    every = 1
    while (rows * every) % (steps * BF16_SUBLANES) and every < steps:
        every *= 2
    return every


def _attention(qkv, weights):
    s = qkv.shape[0]
    t = ATT_BLOCK
    w = SB_HEADS * SB_HEAD_DIM
    steps = s // t
    prev = lambda i: jnp.maximum(i - 1, 0)
    cast_every = tuple(_cast_every(wt.shape[0], steps) for wt, _, _ in weights)
    cast_in_specs = [pl.BlockSpec((wt.shape[0] * e // steps, nc), lambda i, e=e, cb=cb: (i // e, cb))
                     for (wt, nc, cb), e in zip(weights, cast_every)]
    cast_out_specs = [pl.BlockSpec((wt.shape[0] * e // steps, nc), lambda i, e=e: (i // e, 0))
                      for (wt, nc, _), e in zip(weights, cast_every)]
    outs = pl.pallas_call(
        functools.partial(_attn_kernel, cast_every=cast_every),
        grid=(steps,),
        in_specs=[pl.BlockSpec((t, w), lambda i: (i, 0)),
                  pl.BlockSpec((t, w), lambda i: (i, 1)),
                  pl.BlockSpec((t, w), lambda i: (prev(i), 1)),
                  pl.BlockSpec((t, w), lambda i: (i, 2)),
                  pl.BlockSpec((t, w), lambda i: (prev(i), 2)),
                  pl.BlockSpec(memory_space=pl.ANY)] + cast_in_specs,
        out_specs=[pl.BlockSpec((t, w), lambda i: (i, 0))] + cast_out_specs,
        out_shape=[jax.ShapeDtypeStruct((s, w), BF16)]
        + [jax.ShapeDtypeStruct((wt.shape[0], nc), BF16) for wt, nc, _ in weights],
        scratch_shapes=[pltpu.VMEM((t, w), F32), pltpu.VMEM((SB_HEADS, t, 1), F32),
                        pltpu.VMEM((t, w), BF16), pltpu.VMEM((t, w), BF16),
                        pltpu.SemaphoreType.DMA((2,))],
        compiler_params=_params("arbitrary"),
        name="sb_attention",
    )(qkv, qkv, qkv, qkv, qkv, qkv, *[wt for wt, _, _ in weights])
    return outs[0], outs[1:]


def _ffn_kernel(hn_ref, x_hbm, wg_ref, wu_ref, wd_ref, gf_ref, o_ref, x_scr, x_sem, *, final_norm):
    i = pl.program_id(0)
    f = pl.program_id(1)
    tm = o_ref.shape[0]
    x_copy = pltpu.make_async_copy(x_hbm.at[pl.ds(pl.multiple_of(i * tm, tm), tm), :], x_scr, x_sem)

    @pl.when(f == 0)
    def _():
        x_copy.start()
        o_ref[...] = jnp.zeros_like(o_ref)

    hn = hn_ref[...]
    gate = jnp.dot(hn, wg_ref[...], preferred_element_type=F32)
    up = jnp.dot(hn, wu_ref[...], preferred_element_type=F32)
    h = (gate * jax.nn.sigmoid(gate) * up).astype(BF16)
    tn = _pick(o_ref.shape[1], wg_ref.shape[1])
    for c in range(o_ref.shape[1] // tn):
        cols = slice(c * tn, (c + 1) * tn)
        o_ref[:, cols] += jnp.dot(h, wd_ref[:, cols], preferred_element_type=F32)

    @pl.when(f == pl.num_programs(1) - 1)
    def _():
        x_copy.wait()
        for r in range(0, tm, FINAL_ROWS):
            rows = slice(r, r + FINAL_ROWS)
            y = x_scr[rows, :] + o_ref[rows, :]
            if final_norm:
                ms = jnp.mean(y * y, axis=-1, keepdims=True)
                y = y * lax.rsqrt(ms + EPS) * gf_ref[...]
            o_ref[rows, :] = y


def _ffn(hn, x, w_gate_up, w_down, g_final, tm, tf, final_norm):
    s, d = x.shape
    d_ff = w_down.shape[0]
    nf = d_ff // tf
    return pl.pallas_call(
        functools.partial(_ffn_kernel, final_norm=final_norm),
        grid=(s // tm, nf),
        in_specs=[pl.BlockSpec((tm, d), lambda i, f: (i, 0)),
                  pl.BlockSpec(memory_space=pl.ANY),
                  pl.BlockSpec((d, tf), lambda i, f: (0, f)),
                  pl.BlockSpec((d, tf), lambda i, f: (0, nf + f)),
                  pl.BlockSpec((tf, d), lambda i, f: (f, 0)),
                  pl.BlockSpec((1, d), lambda i, f: (0, 0))],
        out_specs=pl.BlockSpec((tm, d), lambda i, f: (i, 0)),
        out_shape=jax.ShapeDtypeStruct((s, d), F32),
        scratch_shapes=[pltpu.VMEM((tm, d), F32), pltpu.SemaphoreType.DMA(())],
        compiler_params=_params("arbitrary", "arbitrary"),
        name="ffn",
    )(hn, x, w_gate_up, w_gate_up, w_down, g_final.reshape(1, d))


def _gated_merge_kernel(xn_ref, a_ref, b_ref, x_ref, *refs):
    o_ref, hn_ref = refs[-2:]
    bg_ref, wa_ref, wb_ref, wo_ref, gn_ref = refs[-7:-2]
    wg_refs = refs[:-7]
    nc = len(wg_refs) // 2
    d = x_ref.shape[1]
    tn = d // nc
    xn = xn_ref[...]
    a = a_ref[...]
    b = b_ref[...]
    parts = []
    for c in range(nc):
        cols = slice(c * tn, (c + 1) * tn)
        g_a = jax.nn.sigmoid(jnp.dot(xn, wg_refs[c][...], preferred_element_type=F32)
                             + bg_ref[:, cols])
        g_b = jax.nn.sigmoid(jnp.dot(xn, wg_refs[nc + c][...], preferred_element_type=F32)
                             + bg_ref[:, d + c * tn:d + (c + 1) * tn])
        y_a = jnp.dot(a, wa_ref[:, cols], preferred_element_type=F32)
        y_b = jnp.dot(b, wb_ref[:, cols], preferred_element_type=F32)
        parts.append((g_a * y_a + g_b * y_b).astype(BF16))
    m = jnp.concatenate(parts, axis=1)
    y = x_ref[...] + jnp.dot(m, wo_ref[...], preferred_element_type=F32)
    o_ref[...] = y
    ms = jnp.mean(y * y, axis=-1, keepdims=True)
    hn_ref[...] = (y * lax.rsqrt(ms + EPS) * gn_ref[...]).astype(BF16)


def _gated_merge(xn, a, b, x, gate_tiles, b_gate, w_a, w_b, w_o, g_norm, tm):
    s, d = x.shape
    wa = a.shape[1]
    wb = b.shape[1]
    resident = dict(pipeline_mode=pl.Buffered(1))
    gate_specs = [pl.BlockSpec(g.shape, lambda i: (0, 0), **resident) for g in gate_tiles]
    return pl.pallas_call(
        _gated_merge_kernel,
        grid=(s // tm,),
        in_specs=[pl.BlockSpec((tm, d), lambda i: (i, 0)),
                  pl.BlockSpec((tm, wa), lambda i: (i, 0)),
                  pl.BlockSpec((tm, wb), lambda i: (i, 0)),
                  pl.BlockSpec((tm, d), lambda i: (i, 0))]
        + gate_specs
        + [pl.BlockSpec((1, 2 * d), lambda i: (0, 0)),
           pl.BlockSpec((wa, d), lambda i: (0, 0), **resident),
           pl.BlockSpec((wb, d), lambda i: (0, 0), **resident),
           pl.BlockSpec((d, d), lambda i: (0, 0), **resident),
           pl.BlockSpec((1, d), lambda i: (0, 0))],
        out_specs=[pl.BlockSpec((tm, d), lambda i: (i, 0)),
                   pl.BlockSpec((tm, d), lambda i: (i, 0))],
        out_shape=[jax.ShapeDtypeStruct((s, d), F32),
                   jax.ShapeDtypeStruct((s, d), BF16)],
        compiler_params=_params("arbitrary"),
        name="gated_merge",
    )(xn, a, b, x, *gate_tiles, b_gate.reshape(1, 2 * d), w_a, w_b, w_o, g_norm.reshape(1, d))


def _pick(n, pref):
    return pref if n % pref == 0 else n


def kernel(x, norm_mix, w_in, b_gate, ln_v_g, ln_v_b, w_s, b_s, w_proj_a, w_proj_b,
           w_out, norm_ffn, w_gate_up, w_down, norm_final):
    batch, seq, d = x.shape
    depth = w_in.shape[0]
    sgu_w = SGU_GROUPS * SGU_GROUP_DIM
    sb_w = SB_HEADS * SB_HEAD_DIM
    assert sgu_w == sb_w
    off_q = 2 * sgu_w
    off_g = off_q + 3 * sb_w
    outs = []
    for b in range(batch):
        h = x.reshape(seq, d) if batch == 1 else x[b]
        for l in range(depth):
            w_first = w_in[l][:, :off_g].astype(BF16)
            xn, a, qkv = _sgu(h, norm_mix[l], w_first, ln_v_g[l], ln_v_b[l], w_s[l], b_s[l],
                              _pick(seq, 512))
            tn = _pick(d, 1024)
            whole = lambda wt: (wt, wt.shape[1], 0)
            gate_cols = [(w_in[l], tn, off_g // tn + k) for k in range(2 * d // tn)]
            o_b, bf = _attention(qkv, [whole(w_proj_a[l]), whole(w_proj_b[l]), whole(w_out[l]),
                                       whole(w_gate_up[l]), whole(w_down[l])] + gate_cols)
            wa_l, wb_l, wo_l, wgu_l, wd_l = bf[:5]
            h, hn = _gated_merge(xn, a, o_b, h, bf[5:], b_gate[l], wa_l, wb_l, wo_l,
                                 norm_ffn[l], _pick(seq, 256))
            h = _ffn(hn, h, wgu_l, wd_l, norm_final, _pick(seq, 1024),
                     _pick(w_down.shape[1], 512), final_norm=(l == depth - 1))
        outs.append(h)
    return outs[0].reshape(x.shape) if batch == 1 else jnp.stack(outs)
```

```python
import functools
import math

import jax
import jax.numpy as jnp
from jax import lax
from jax.experimental import pallas as pl
from jax.experimental.pallas import tpu as pltpu

EPS = 1e-6
CHUNK = 128
SGU_GROUPS = 8
SGU_GROUP_DIM = 128
SB_HEADS = 8
SB_HEAD_DIM = 128
ATT_BLOCK = 256
SGU_ROWS = 512
CAST_STAGE_ROWS = 64
MERGE_ROWS = 256
GATE_COLS = 1024
FFN_ROWS = 1024
FFN_COLS = 512
BF16_SUBLANES = 16
FINAL_ROWS = 16
STICK_USED_F32_ZERO = 150.0
MASKED_SCORE = -1e30
SOFTPLUS_LINEAR_ABOVE = 90.0
LOG2_E = math.log2(math.e)
VMEM_LIMIT_BYTES = 60 * 1024 * 1024

F32 = jnp.float32
BF16 = jnp.bfloat16


def _params(*sem):
    return pltpu.CompilerParams(dimension_semantics=sem, vmem_limit_bytes=VMEM_LIMIT_BYTES)


def _softplus_bits(z):
    return jnp.where(z > SOFTPLUS_LINEAR_ABOVE, z, jnp.log(1.0 + jnp.exp2(z)) * LOG2_E)


def _gelu(x):
    return 0.5 * x * (1.0 + lax.erf(x * math.sqrt(0.5)))


def _sgu_kernel(x_ref, gn_ref, w_hbm, lng_ref, lnb_ref, ws_ref, bs_ref,
                xn_ref, o_ref, qkv_ref, w_ref, stage, sems):
    width = o_ref.shape[1]

    @pl.when(pl.program_id(0) == 0)
    def _():
        rows = stage.shape[1]
        chunks = w_ref.shape[0] // rows

        def chunk_copy(c, slot):
            return pltpu.make_async_copy(
                w_hbm.at[pl.ds(pl.multiple_of(c * rows, rows), rows), pl.ds(0, w_ref.shape[1])],
                stage.at[slot], sems.at[slot])

        chunk_copy(0, 0).start()

        def body(c, carry):
            slot = c % 2

            @pl.when(c + 1 < chunks)
            def _():
                chunk_copy(c + 1, 1 - slot).start()

            chunk_copy(c, slot).wait()
            w_ref[pl.ds(pl.multiple_of(c * rows, rows), rows), :] = stage[slot].astype(BF16)
            return carry

        lax.fori_loop(0, chunks, body, 0)

    def project(block):
        cols = slice(block * width, (block + 1) * width)
        return jnp.dot(xn, w_ref[:, cols], preferred_element_type=F32)

    x = x_ref[...]
    ms = jnp.mean(x * x, axis=-1, keepdims=True)
    xn = (x * lax.rsqrt(ms + EPS) * gn_ref[...]).astype(BF16)
    xn_ref[...] = xn
    v = _gelu(project(1))
    qkv_ref[:, :width] = project(2).astype(BF16)
    mu = jnp.mean(v, axis=-1, keepdims=True)
    cen = v - mu
    var = jnp.mean(cen * cen, axis=-1, keepdims=True)
    vln = (cen * lax.rsqrt(var + EPS) * lng_ref[...] + lnb_ref[...]).astype(BF16)
    u = _gelu(project(0))
    qkv_ref[:, width:2 * width] = project(3).astype(BF16)
    qkv_ref[:, 2 * width:] = project(4).astype(BF16)
    t_idx = lax.broadcasted_iota(jnp.int32, (CHUNK, CHUNK), 0)
    s_idx = lax.broadcasted_iota(jnp.int32, (CHUNK, CHUNK), 1)
    causal = s_idx <= t_idx
    for g in range(SGU_GROUPS):
        wm = jnp.where(causal, ws_ref[g], 0.0).astype(BF16)
        cols = slice(g * SGU_GROUP_DIM, (g + 1) * SGU_GROUP_DIM)
        for c in range(o_ref.shape[0] // CHUNK):
            rows = slice(c * CHUNK, (c + 1) * CHUNK)
            mixed = jnp.dot(wm, vln[rows, cols], preferred_element_type=F32) + bs_ref[g]
            o_ref[rows, cols] = (u[rows, cols] * mixed).astype(o_ref.dtype)


def _sgu(x, g_norm, w_in, ln_g, ln_b, w_s, b_s, tm):
    s, d = x.shape
    width = SGU_GROUPS * SGU_GROUP_DIM
    return pl.pallas_call(
        _sgu_kernel,
        grid=(s // tm,),
        in_specs=[pl.BlockSpec((tm, d), lambda i: (i, 0)),
                  pl.BlockSpec((1, d), lambda i: (0, 0)),
                  pl.BlockSpec(memory_space=pl.ANY),
                  pl.BlockSpec((1, width), lambda i: (0, 0)),
                  pl.BlockSpec((1, width), lambda i: (0, 0)),
                  pl.BlockSpec(w_s.shape, lambda i: (0, 0, 0)),
                  pl.BlockSpec((SGU_GROUPS, CHUNK, 1), lambda i: (0, 0, 0))],
        out_specs=[pl.BlockSpec((tm, d), lambda i: (i, 0)),
                   pl.BlockSpec((tm, width), lambda i: (i, 0)),
                   pl.BlockSpec((tm, 3 * width), lambda i: (i, 0))],
        out_shape=[jax.ShapeDtypeStruct((s, d), BF16),
                   jax.ShapeDtypeStruct((s, width), BF16),
                   jax.ShapeDtypeStruct((s, 3 * width), BF16)],
        scratch_shapes=[pltpu.VMEM((d, 5 * width), BF16),
                        pltpu.VMEM((2, _pick(d, CAST_STAGE_ROWS), 5 * width), F32),
                        pltpu.SemaphoreType.DMA((2,))],
        compiler_params=_params("arbitrary"),
        name="sgu_qkv",
    )(x, g_norm.reshape(1, d), w_in, ln_g.reshape(1, width), ln_b.reshape(1, width), w_s,
      b_s.reshape(SGU_GROUPS, CHUNK, 1))


def _attn_kernel(*refs, cast_every):
    n_cast = len(cast_every)
    q_ref, kc_ref, kp_ref, vc_ref, vp_ref, qkv_hbm = refs[:6]
    cast_in = refs[6:6 + n_cast]
    o_ref = refs[6 + n_cast]
    cast_out = refs[7 + n_cast:7 + 2 * n_cast]
    acc_ref, used_ref, kbuf, vbuf, sems = refs[7 + 2 * n_cast:]
    t = kp_ref.shape[0]
    dh = SB_HEAD_DIM
    width = q_ref.shape[1]
    heads = width // dh
    pair = pl.program_id(0)
    scale = LOG2_E / math.sqrt(dh)
    row = lax.broadcasted_iota(jnp.int32, (t, t), 0)
    col = lax.broadcasted_iota(jnp.int32, (t, t), 1)
    later = (row > col).astype(BF16)
    causal = col < row
    no_prev = jnp.where(pair > 0, 0.0, -MASKED_SCORE)
    first, second = slice(0, t), slice(t, 2 * t)

    for src, dst, every in zip(cast_in, cast_out, cast_every):
        if every == 1:
            dst[...] = src[...].astype(BF16)
        else:
            @pl.when(pair % every == 0)
            def _(src=src, dst=dst):
                dst[...] = src[...].astype(BF16)

    def scores(q, k):
        return lax.dot_general(q, k, (((1,), (1,)), ((), ())), preferred_element_type=F32) * scale

    def log_weights(z):
        used = _softplus_bits(z)
        logw = z - used - jnp.dot(used.astype(BF16), later, preferred_element_type=F32)
        return logw, jnp.sum(used, axis=-1, keepdims=True)

    least = [None, None]
    for h in range(heads):
        cols = slice(h * dh, (h + 1) * dh)
        z_a = scores(q_ref[:, cols], kc_ref[first, cols])
        z_ap = scores(q_ref[first, cols], kp_ref[:, cols])
        z_bb = scores(q_ref[second, cols], kc_ref[second, cols])
        logw, used_rows = log_weights(jnp.concatenate(
            [jnp.where(causal, z_a[first], MASKED_SCORE), z_ap,
             jnp.where(causal, z_bb, MASKED_SCORE), z_a[second]], axis=0))
        values = ((vc_ref[first, cols], vp_ref[:, cols]), (vc_ref[second, cols], vc_ref[first, cols]))
        for n, (rows, (v_diag, v_prev)) in enumerate(zip((first, second), values)):
            rows_d = slice(2 * n * t, (2 * n + 1) * t)
            rows_p = slice((2 * n + 1) * t, (2 * n + 2) * t)
            used_d, used_p = used_rows[rows_d], used_rows[rows_p]
            before = used_d + no_prev if n == 0 else used_d
            w = jnp.concatenate([jnp.exp2(logw[rows_d]), jnp.exp2(logw[rows_p] - before)],
                                axis=1).astype(BF16)
            acc_ref[rows, cols] = jnp.dot(w, jnp.concatenate([v_diag, v_prev], axis=0),
                                          preferred_element_type=F32)
            used = used_d + used_p
            used_ref[h, rows] = used
            least[n] = used if least[n] is None else jnp.minimum(least[n], used)

    def walk(rows, first_block, low):
        def cond(state):
            kb, low = state
            return jnp.logical_and(kb >= 0, low <= STICK_USED_F32_ZERO)

        def body(state):
            kb, _ = state
            src_rows = pl.ds(pl.multiple_of(kb * t, t), t)
            k_copy = pltpu.make_async_copy(qkv_hbm.at[src_rows, pl.ds(width, width)], kbuf,
                                           sems.at[0])
            v_copy = pltpu.make_async_copy(qkv_hbm.at[src_rows, pl.ds(2 * width, width)], vbuf,
                                           sems.at[1])
            k_copy.start()
            v_copy.start()
            k_copy.wait()
            v_copy.wait()
            least = None
            for h in range(heads):
                cols = slice(h * dh, (h + 1) * dh)
                logw, used_b = log_weights(scores(q_ref[rows, cols], kbuf[:, cols]))
                used = used_ref[h, rows]
                w = jnp.exp2(logw - used).astype(BF16)
                acc_ref[rows, cols] += jnp.dot(w, vbuf[:, cols], preferred_element_type=F32)
                used = used + used_b
                used_ref[h, rows] = used
                least = used if least is None else jnp.minimum(least, used)
            return kb - 1, jnp.min(least)

        lax.while_loop(cond, body, (first_block, low))

    walk(first, 2 * pair - 2, jnp.min(least[0]))
    walk(second, 2 * pair - 1, jnp.min(least[1]))
    o_ref[...] = acc_ref[...].astype(o_ref.dtype)


def _cast_every(rows, steps):
    every = 1
    while (rows * every) % (steps * BF16_SUBLANES) and every < steps:
        every *= 2
    return every


def _attention(qkv, weights):
    s = qkv.shape[0]
    t = ATT_BLOCK
    w = SB_HEADS * SB_HEAD_DIM
    steps = s // (2 * t)
    before = lambda i: jnp.maximum(2 * i - 1, 0)
    cast_every = tuple(_cast_every(wt.shape[0], steps) for wt, _, _ in weights)
    assert all(steps % e == 0 for e in cast_every)
    cast_in_specs = [pl.BlockSpec((wt.shape[0] * e // steps, nc), lambda i, e=e, cb=cb: (i // e, cb))
                     for (wt, nc, cb), e in zip(weights, cast_every)]
    cast_out_specs = [pl.BlockSpec((wt.shape[0] * e // steps, nc), lambda i, e=e: (i // e, 0))
                      for (wt, nc, _), e in zip(weights, cast_every)]
    outs = pl.pallas_call(
        functools.partial(_attn_kernel, cast_every=cast_every),
        grid=(steps,),
        in_specs=[pl.BlockSpec((2 * t, w), lambda i: (i, 0)),
                  pl.BlockSpec((2 * t, w), lambda i: (i, 1)),
                  pl.BlockSpec((t, w), lambda i: (before(i), 1)),
                  pl.BlockSpec((2 * t, w), lambda i: (i, 2)),
                  pl.BlockSpec((t, w), lambda i: (before(i), 2)),
                  pl.BlockSpec(memory_space=pl.ANY)] + cast_in_specs,
        out_specs=[pl.BlockSpec((2 * t, w), lambda i: (i, 0))] + cast_out_specs,
        out_shape=[jax.ShapeDtypeStruct((s, w), BF16)]
        + [jax.ShapeDtypeStruct((wt.shape[0], nc), BF16) for wt, nc, _ in weights],
        scratch_shapes=[pltpu.VMEM((2 * t, w), F32), pltpu.VMEM((SB_HEADS, 2 * t, 1), F32),
                        pltpu.VMEM((t, w), BF16), pltpu.VMEM((t, w), BF16),
                        pltpu.SemaphoreType.DMA((2,))],
        compiler_params=_params("arbitrary"),
        name="sb_attention",
    )(qkv, qkv, qkv, qkv, qkv, qkv, *[wt for wt, _, _ in weights])
    return outs[0], outs[1:]


def _ffn_kernel(hn_ref, x_hbm, wg_ref, wu_ref, wd_ref, gf_ref, o_ref, x_scr, x_sem, *, final_norm):
    i = pl.program_id(0)
    f = pl.program_id(1)
    tm = o_ref.shape[0]
    x_copy = pltpu.make_async_copy(x_hbm.at[pl.ds(pl.multiple_of(i * tm, tm), tm), :], x_scr, x_sem)

    @pl.when(f == 0)
    def _():
        x_copy.start()
        o_ref[...] = jnp.zeros_like(o_ref)

    hn = hn_ref[...]
    gate = jnp.dot(hn, wg_ref[...], preferred_element_type=F32)
    up = jnp.dot(hn, wu_ref[...], preferred_element_type=F32)
    h = (gate * jax.nn.sigmoid(gate) * up).astype(BF16)
    tn = _pick(o_ref.shape[1], wg_ref.shape[1])
    for c in range(o_ref.shape[1] // tn):
        cols = slice(c * tn, (c + 1) * tn)
        o_ref[:, cols] += jnp.dot(h, wd_ref[:, cols], preferred_element_type=F32)

    @pl.when(f == pl.num_programs(1) - 1)
    def _():
        x_copy.wait()
        for r in range(0, tm, FINAL_ROWS):
            rows = slice(r, r + FINAL_ROWS)
            y = x_scr[rows, :] + o_ref[rows, :]
            if final_norm:
                ms = jnp.mean(y * y, axis=-1, keepdims=True)
                y = y * lax.rsqrt(ms + EPS) * gf_ref[...]
            o_ref[rows, :] = y


def _ffn(hn, x, w_gate_up, w_down, g_final, tm, tf, final_norm):
    s, d = x.shape
    d_ff = w_down.shape[0]
    nf = d_ff // tf
    return pl.pallas_call(
        functools.partial(_ffn_kernel, final_norm=final_norm),
        grid=(s // tm, nf),
        in_specs=[pl.BlockSpec((tm, d), lambda i, f: (i, 0)),
                  pl.BlockSpec(memory_space=pl.ANY),
                  pl.BlockSpec((d, tf), lambda i, f: (0, f)),
                  pl.BlockSpec((d, tf), lambda i, f: (0, nf + f)),
                  pl.BlockSpec((tf, d), lambda i, f: (f, 0)),
                  pl.BlockSpec((1, d), lambda i, f: (0, 0))],
        out_specs=pl.BlockSpec((tm, d), lambda i, f: (i, 0)),
        out_shape=jax.ShapeDtypeStruct((s, d), F32),
        scratch_shapes=[pltpu.VMEM((tm, d), F32), pltpu.SemaphoreType.DMA(())],
        compiler_params=_params("arbitrary", "arbitrary"),
        name="ffn",
    )(hn, x, w_gate_up, w_gate_up, w_down, g_final.reshape(1, d))


def _gated_merge_kernel(xn_ref, a_ref, b_ref, x_ref, *refs):
    o_ref, hn_ref = refs[-2:]
    bg_ref, wa_ref, wb_ref, wo_ref, gn_ref = refs[-7:-2]
    wg_refs = refs[:-7]
    nc = len(wg_refs) // 2
    d = x_ref.shape[1]
    tn = d // nc
    xn = xn_ref[...]
    a = a_ref[...]
    b = b_ref[...]
    parts = []
    for c in range(nc):
        cols = slice(c * tn, (c + 1) * tn)
        g_a = jax.nn.sigmoid(jnp.dot(xn, wg_refs[c][...], preferred_element_type=F32)
                             + bg_ref[:, cols])
        g_b = jax.nn.sigmoid(jnp.dot(xn, wg_refs[nc + c][...], preferred_element_type=F32)
                             + bg_ref[:, d + c * tn:d + (c + 1) * tn])
        y_a = jnp.dot(a, wa_ref[:, cols], preferred_element_type=F32)
        y_b = jnp.dot(b, wb_ref[:, cols], preferred_element_type=F32)
        parts.append((g_a * y_a + g_b * y_b).astype(BF16))
    m = jnp.concatenate(parts, axis=1)
    y = x_ref[...] + jnp.dot(m, wo_ref[...], preferred_element_type=F32)
    o_ref[...] = y
    ms = jnp.mean(y * y, axis=-1, keepdims=True)
    hn_ref[...] = (y * lax.rsqrt(ms + EPS) * gn_ref[...]).astype(BF16)


def _gated_merge(xn, a, b, x, gate_tiles, b_gate, w_a, w_b, w_o, g_norm, tm):
    s, d = x.shape
    wa = a.shape[1]
    wb = b.shape[1]
    resident = dict(pipeline_mode=pl.Buffered(1))
    gate_specs = [pl.BlockSpec(g.shape, lambda i: (0, 0), **resident) for g in gate_tiles]
    return pl.pallas_call(
        _gated_merge_kernel,
        grid=(s // tm,),
        in_specs=[pl.BlockSpec((tm, d), lambda i: (i, 0)),
                  pl.BlockSpec((tm, wa), lambda i: (i, 0)),
                  pl.BlockSpec((tm, wb), lambda i: (i, 0)),
                  pl.BlockSpec((tm, d), lambda i: (i, 0))]
        + gate_specs
        + [pl.BlockSpec((1, 2 * d), lambda i: (0, 0)),
           pl.BlockSpec((wa, d), lambda i: (0, 0), **resident),
           pl.BlockSpec((wb, d), lambda i: (0, 0), **resident),
           pl.BlockSpec((d, d), lambda i: (0, 0), **resident),
           pl.BlockSpec((1, d), lambda i: (0, 0))],
        out_specs=[pl.BlockSpec((tm, d), lambda i: (i, 0)),
                   pl.BlockSpec((tm, d), lambda i: (i, 0))],
        out_shape=[jax.ShapeDtypeStruct((s, d), F32),
                   jax.ShapeDtypeStruct((s, d), BF16)],
        compiler_params=_params("arbitrary"),
        name="gated_merge",
    )(xn, a, b, x, *gate_tiles, b_gate.reshape(1, 2 * d), w_a, w_b, w_o, g_norm.reshape(1, d))


def _pick(n, pref):
    return pref if n % pref == 0 else n


def kernel(x, norm_mix, w_in, b_gate, ln_v_g, ln_v_b, w_s, b_s, w_proj_a, w_proj_b,
           w_out, norm_ffn, w_gate_up, w_down, norm_final):
    batch, seq, d = x.shape
    depth = w_in.shape[0]
    sgu_w = SGU_GROUPS * SGU_GROUP_DIM
    sb_w = SB_HEADS * SB_HEAD_DIM
    assert sgu_w == sb_w
    off_q = 2 * sgu_w
    off_g = off_q + 3 * sb_w
    outs = []
    for b in range(batch):
        h = x.reshape(seq, d) if batch == 1 else x[b]
        for l in range(depth):
            xn, a, qkv = _sgu(h, norm_mix[l], w_in[l], ln_v_g[l], ln_v_b[l], w_s[l], b_s[l],
                              _pick(seq, SGU_ROWS))
            tn = _pick(d, GATE_COLS)
            whole = lambda wt: (wt, wt.shape[1], 0)
            gate_cols = [(w_in[l], tn, off_g // tn + k) for k in range(2 * d // tn)]
            o_b, bf = _attention(qkv, [whole(w_proj_a[l]), whole(w_proj_b[l]), whole(w_out[l]),
                                       whole(w_gate_up[l]), whole(w_down[l])] + gate_cols)
            wa_l, wb_l, wo_l, wgu_l, wd_l = bf[:5]
            h, hn = _gated_merge(xn, a, o_b, h, bf[5:], b_gate[l], wa_l, wb_l, wo_l,
                                 norm_ffn[l], _pick(seq, MERGE_ROWS))
            h = _ffn(hn, h, wgu_l, wd_l, norm_final, _pick(seq, FFN_ROWS),
                     _pick(w_down.shape[1], FFN_COLS), final_norm=(l == depth - 1))
        outs.append(h)
    return outs[0].reshape(x.shape) if batch == 1 else jnp.stack(outs)
```

```python
import functools
import math

import jax
import jax.numpy as jnp
from jax import lax
from jax.experimental import pallas as pl
from jax.experimental.pallas import tpu as pltpu

EPS = 1e-6
CHUNK = 128
SGU_GROUPS = 8
SGU_GROUP_DIM = 128
SB_HEADS = 8
SB_HEAD_DIM = 128
ATT_BLOCK = 256
SGU_ROWS = 512
MERGE_ROWS = 256
GATE_COLS = 1024
FFN_UP_ROWS = 1024
FFN_COLS = 512
FFN_DOWN_ROWS = 512
BF16_SUBLANES = 16
STICK_USED_F32_ZERO = 150.0
MASKED_SCORE = -1e30
SOFTPLUS_LINEAR_ABOVE = 90.0
LOG2_E = math.log2(math.e)
VMEM_LIMIT_BYTES = 60 * 1024 * 1024

F32 = jnp.float32
BF16 = jnp.bfloat16


def _params(*sem):
    return pltpu.CompilerParams(dimension_semantics=sem, vmem_limit_bytes=VMEM_LIMIT_BYTES)


def _softplus_bits(z):
    return jnp.where(z > SOFTPLUS_LINEAR_ABOVE, z, jnp.log(1.0 + jnp.exp2(z)) * LOG2_E)


def _gelu(x):
    return 0.5 * x * (1.0 + lax.erf(x * math.sqrt(0.5)))


def _sgu_kernel(x_ref, gn_ref, w_ref, wq_ref, wk_ref, wv_ref, lng_ref, lnb_ref, ws_ref, bs_ref,
                xn_ref, o_ref, qkv_ref):
    width = o_ref.shape[1]
    x = x_ref[...]
    ms = jnp.mean(x * x, axis=-1, keepdims=True)
    xn = (x * lax.rsqrt(ms + EPS) * gn_ref[...]).astype(BF16)
    xn_ref[...] = xn
    v = _gelu(jnp.dot(xn, w_ref[:, width:], preferred_element_type=F32))
    qkv_ref[:, :width] = jnp.dot(xn, wq_ref[...], preferred_element_type=F32).astype(BF16)
    mu = jnp.mean(v, axis=-1, keepdims=True)
    cen = v - mu
    var = jnp.mean(cen * cen, axis=-1, keepdims=True)
    vln = (cen * lax.rsqrt(var + EPS) * lng_ref[...] + lnb_ref[...]).astype(BF16)
    u = _gelu(jnp.dot(xn, w_ref[:, :width], preferred_element_type=F32))
    qkv_ref[:, width:2 * width] = jnp.dot(xn, wk_ref[...], preferred_element_type=F32).astype(BF16)
    qkv_ref[:, 2 * width:] = jnp.dot(xn, wv_ref[...], preferred_element_type=F32).astype(BF16)
    t_idx = lax.broadcasted_iota(jnp.int32, (CHUNK, CHUNK), 0)
    s_idx = lax.broadcasted_iota(jnp.int32, (CHUNK, CHUNK), 1)
    causal = s_idx <= t_idx
    for g in range(SGU_GROUPS):
        wm = jnp.where(causal, ws_ref[g], 0.0).astype(BF16)
        cols = slice(g * SGU_GROUP_DIM, (g + 1) * SGU_GROUP_DIM)
        for c in range(o_ref.shape[0] // CHUNK):
            rows = slice(c * CHUNK, (c + 1) * CHUNK)
            mixed = jnp.dot(wm, vln[rows, cols], preferred_element_type=F32) + bs_ref[g]
            o_ref[rows, cols] = (u[rows, cols] * mixed).astype(o_ref.dtype)


def _sgu(x, g_norm, w_in, ln_g, ln_b, w_s, b_s, tm):
    s, d = x.shape
    width = SGU_GROUPS * SGU_GROUP_DIM
    resident = dict(pipeline_mode=pl.Buffered(1))
    return pl.pallas_call(
        _sgu_kernel,
        grid=(s // tm,),
        in_specs=[pl.BlockSpec((tm, d), lambda i: (i, 0)),
                  pl.BlockSpec((1, d), lambda i: (0, 0)),
                  pl.BlockSpec((d, 2 * width), lambda i: (0, 0), **resident),
                  pl.BlockSpec((d, width), lambda i: (0, 2), **resident),
                  pl.BlockSpec((d, width), lambda i: (0, 3), **resident),
                  pl.BlockSpec((d, width), lambda i: (0, 4), **resident),
                  pl.BlockSpec((1, width), lambda i: (0, 0)),
                  pl.BlockSpec((1, width), lambda i: (0, 0)),
                  pl.BlockSpec(w_s.shape, lambda i: (0, 0, 0)),
                  pl.BlockSpec((SGU_GROUPS, CHUNK, 1), lambda i: (0, 0, 0))],
        out_specs=[pl.BlockSpec((tm, d), lambda i: (i, 0)),
                   pl.BlockSpec((tm, width), lambda i: (i, 0)),
                   pl.BlockSpec((tm, 3 * width), lambda i: (i, 0))],
        out_shape=[jax.ShapeDtypeStruct((s, d), BF16),
                   jax.ShapeDtypeStruct((s, width), BF16),
                   jax.ShapeDtypeStruct((s, 3 * width), BF16)],
        compiler_params=_params("arbitrary"),
        name="sgu_qkv",
    )(x, g_norm.reshape(1, d), w_in, w_in, w_in, w_in, ln_g.reshape(1, width),
      ln_b.reshape(1, width), w_s, b_s.reshape(SGU_GROUPS, CHUNK, 1))


def _attn_kernel(*refs, cast_every):
    n_cast = len(cast_every)
    q_ref, kc_ref, kp_ref, vc_ref, vp_ref, qkv_hbm = refs[:6]
    cast_in = refs[6:6 + n_cast]
    o_ref = refs[6 + n_cast]
    cast_out = refs[7 + n_cast:7 + 2 * n_cast]
    acc_ref, used_ref, kbuf, vbuf, sems = refs[7 + 2 * n_cast:]
    t = kp_ref.shape[0]
    dh = SB_HEAD_DIM
    width = q_ref.shape[1]
    heads = width // dh
    pair = pl.program_id(0)
    scale = LOG2_E / math.sqrt(dh)
    row = lax.broadcasted_iota(jnp.int32, (t, t), 0)
    col = lax.broadcasted_iota(jnp.int32, (t, t), 1)
    later = (row > col).astype(BF16)
    causal = col < row
    no_prev = jnp.where(pair > 0, 0.0, -MASKED_SCORE)
    first, second = slice(0, t), slice(t, 2 * t)

    for src, dst, every in zip(cast_in, cast_out, cast_every):
        if every == 1:
            dst[...] = src[...].astype(BF16)
        else:
            @pl.when(pair % every == 0)
            def _(src=src, dst=dst):
                dst[...] = src[...].astype(BF16)

    def scores(q, k):
        return lax.dot_general(q, k, (((1,), (1,)), ((), ())), preferred_element_type=F32) * scale

    def log_weights(z):
        used = _softplus_bits(z)
        logw = z - used - jnp.dot(used.astype(BF16), later, preferred_element_type=F32)
        return logw, jnp.sum(used, axis=-1, keepdims=True)

    least = [None, None]
    for h in range(heads):
        cols = slice(h * dh, (h + 1) * dh)
        z_a = scores(q_ref[:, cols], kc_ref[first, cols])
        z_ap = scores(q_ref[first, cols], kp_ref[:, cols])
        z_bb = scores(q_ref[second, cols], kc_ref[second, cols])
        logw, used_rows = log_weights(jnp.concatenate(
            [jnp.where(causal, z_a[first], MASKED_SCORE), z_ap,
             jnp.where(causal, z_bb, MASKED_SCORE), z_a[second]], axis=0))
        values = ((vc_ref[first, cols], vp_ref[:, cols]), (vc_ref[second, cols], vc_ref[first, cols]))
        for n, (rows, (v_diag, v_prev)) in enumerate(zip((first, second), values)):
            rows_d = slice(2 * n * t, (2 * n + 1) * t)
            rows_p = slice((2 * n + 1) * t, (2 * n + 2) * t)
            used_d, used_p = used_rows[rows_d], used_rows[rows_p]
            before = used_d + no_prev if n == 0 else used_d
            w = jnp.concatenate([jnp.exp2(logw[rows_d]), jnp.exp2(logw[rows_p] - before)],
                                axis=1).astype(BF16)
            acc_ref[rows, cols] = jnp.dot(w, jnp.concatenate([v_diag, v_prev], axis=0),
                                          preferred_element_type=F32)
            used = used_d + used_p
            used_ref[h, rows] = used
            least[n] = used if least[n] is None else jnp.minimum(least[n], used)

    def walk(rows, first_block, low):
        def cond(state):
            kb, low = state
            return jnp.logical_and(kb >= 0, low <= STICK_USED_F32_ZERO)

        def body(state):
            kb, _ = state
            src_rows = pl.ds(pl.multiple_of(kb * t, t), t)
            k_copy = pltpu.make_async_copy(qkv_hbm.at[src_rows, pl.ds(width, width)], kbuf,
                                           sems.at[0])
            v_copy = pltpu.make_async_copy(qkv_hbm.at[src_rows, pl.ds(2 * width, width)], vbuf,
                                           sems.at[1])
            k_copy.start()
            v_copy.start()
            k_copy.wait()
            v_copy.wait()
            least = None
            for h in range(heads):
                cols = slice(h * dh, (h + 1) * dh)
                logw, used_b = log_weights(scores(q_ref[rows, cols], kbuf[:, cols]))
                used = used_ref[h, rows]
                w = jnp.exp2(logw - used).astype(BF16)
                acc_ref[rows, cols] += jnp.dot(w, vbuf[:, cols], preferred_element_type=F32)
                used = used + used_b
                used_ref[h, rows] = used
                least = used if least is None else jnp.minimum(least, used)
            return kb - 1, jnp.min(least)

        lax.while_loop(cond, body, (first_block, low))

    walk(first, 2 * pair - 2, jnp.min(least[0]))
    walk(second, 2 * pair - 1, jnp.min(least[1]))
    o_ref[...] = acc_ref[...].astype(o_ref.dtype)


def _cast_every(rows, steps):
    every = 1
    while (rows * every) % (steps * BF16_SUBLANES) and every < steps:
        every *= 2
    return every


def _attention(qkv, weights):
    s = qkv.shape[0]
    t = ATT_BLOCK
    w = SB_HEADS * SB_HEAD_DIM
    steps = s // (2 * t)
    before = lambda i: jnp.maximum(2 * i - 1, 0)
    cast_every = tuple(_cast_every(wt.shape[0], steps) for wt, _, _ in weights)
    assert all(steps % e == 0 for e in cast_every)
    cast_in_specs = [pl.BlockSpec((wt.shape[0] * e // steps, nc), lambda i, e=e, cb=cb: (i // e, cb))
                     for (wt, nc, cb), e in zip(weights, cast_every)]
    cast_out_specs = [pl.BlockSpec((wt.shape[0] * e // steps, nc), lambda i, e=e: (i // e, 0))
                      for (wt, nc, _), e in zip(weights, cast_every)]
    outs = pl.pallas_call(
        functools.partial(_attn_kernel, cast_every=cast_every),
        grid=(steps,),
        in_specs=[pl.BlockSpec((2 * t, w), lambda i: (i, 0)),
                  pl.BlockSpec((2 * t, w), lambda i: (i, 1)),
                  pl.BlockSpec((t, w), lambda i: (before(i), 1)),
                  pl.BlockSpec((2 * t, w), lambda i: (i, 2)),
                  pl.BlockSpec((t, w), lambda i: (before(i), 2)),
                  pl.BlockSpec(memory_space=pl.ANY)] + cast_in_specs,
        out_specs=[pl.BlockSpec((2 * t, w), lambda i: (i, 0))] + cast_out_specs,
        out_shape=[jax.ShapeDtypeStruct((s, w), BF16)]
        + [jax.ShapeDtypeStruct((wt.shape[0], nc), BF16) for wt, nc, _ in weights],
        scratch_shapes=[pltpu.VMEM((2 * t, w), F32), pltpu.VMEM((SB_HEADS, 2 * t, 1), F32),
                        pltpu.VMEM((t, w), BF16), pltpu.VMEM((t, w), BF16),
                        pltpu.SemaphoreType.DMA((2,))],
        compiler_params=_params("arbitrary"),
        name="sb_attention",
    )(qkv, qkv, qkv, qkv, qkv, qkv, *[wt for wt, _, _ in weights])
    return outs[0], outs[1:]


def _ffn_up_kernel(hn_ref, wg_ref, wu_ref, h_ref):
    hn = hn_ref[...]
    gate = jnp.dot(hn, wg_ref[...], preferred_element_type=F32)
    up = jnp.dot(hn, wu_ref[...], preferred_element_type=F32)
    h_ref[...] = (gate * jax.nn.sigmoid(gate) * up).astype(BF16)


def _ffn_down_kernel(h_ref, x_ref, wd_ref, gf_ref, o_ref, *, final_norm):
    y = x_ref[...] + jnp.dot(h_ref[...], wd_ref[...], preferred_element_type=F32)
    if final_norm:
        ms = jnp.mean(y * y, axis=-1, keepdims=True)
        y = y * lax.rsqrt(ms + EPS) * gf_ref[...]
    o_ref[...] = y


def _ffn(hn, x, w_gate_up, w_down, g_final, tm_up, tf, tm_down, final_norm):
    s, d = x.shape
    d_ff = w_down.shape[0]
    nf = d_ff // tf
    h = pl.pallas_call(
        _ffn_up_kernel,
        grid=(s // tm_up, nf),
        in_specs=[pl.BlockSpec((tm_up, d), lambda i, f: (i, 0)),
                  pl.BlockSpec((d, tf), lambda i, f: (0, f)),
                  pl.BlockSpec((d, tf), lambda i, f: (0, nf + f))],
        out_specs=pl.BlockSpec((tm_up, tf), lambda i, f: (i, f)),
        out_shape=jax.ShapeDtypeStruct((s, d_ff), BF16),
        compiler_params=_params("arbitrary", "arbitrary"),
        name="ffn_up",
    )(hn, w_gate_up, w_gate_up)
    return pl.pallas_call(
        functools.partial(_ffn_down_kernel, final_norm=final_norm),
        grid=(s // tm_down,),
        in_specs=[pl.BlockSpec((tm_down, d_ff), lambda i: (i, 0)),
                  pl.BlockSpec((tm_down, d), lambda i: (i, 0)),
                  pl.BlockSpec((d_ff, d), lambda i: (0, 0), pipeline_mode=pl.Buffered(1)),
                  pl.BlockSpec((1, d), lambda i: (0, 0))],
        out_specs=pl.BlockSpec((tm_down, d), lambda i: (i, 0)),
        out_shape=jax.ShapeDtypeStruct((s, d), F32),
        compiler_params=_params("arbitrary"),
        name="ffn_down",
    )(h, x, w_down, g_final.reshape(1, d))


def _gated_merge_kernel(xn_ref, a_ref, b_ref, x_ref, *refs):
    o_ref, hn_ref = refs[-2:]
    bg_ref, wa_ref, wb_ref, wo_ref, gn_ref = refs[-7:-2]
    wg_refs = refs[:-7]
    nc = len(wg_refs) // 2
    d = x_ref.shape[1]
    tn = d // nc
    xn = xn_ref[...]
    a = a_ref[...]
    b = b_ref[...]
    parts = []
    for c in range(nc):
        cols = slice(c * tn, (c + 1) * tn)
        g_a = jax.nn.sigmoid(jnp.dot(xn, wg_refs[c][...], preferred_element_type=F32)
                             + bg_ref[:, cols])
        g_b = jax.nn.sigmoid(jnp.dot(xn, wg_refs[nc + c][...], preferred_element_type=F32)
                             + bg_ref[:, d + c * tn:d + (c + 1) * tn])
        y_a = jnp.dot(a, wa_ref[:, cols], preferred_element_type=F32)
        y_b = jnp.dot(b, wb_ref[:, cols], preferred_element_type=F32)
        parts.append((g_a * y_a + g_b * y_b).astype(BF16))
    m = jnp.concatenate(parts, axis=1)
    y = x_ref[...] + jnp.dot(m, wo_ref[...], preferred_element_type=F32)
    o_ref[...] = y
    ms = jnp.mean(y * y, axis=-1, keepdims=True)
    hn_ref[...] = (y * lax.rsqrt(ms + EPS) * gn_ref[...]).astype(BF16)


def _gated_merge(xn, a, b, x, gate_tiles, b_gate, w_a, w_b, w_o, g_norm, tm):
    s, d = x.shape
    wa = a.shape[1]
    wb = b.shape[1]
    resident = dict(pipeline_mode=pl.Buffered(1))
    gate_specs = [pl.BlockSpec(g.shape, lambda i: (0, 0), **resident) for g in gate_tiles]
    return pl.pallas_call(
        _gated_merge_kernel,
        grid=(s // tm,),
        in_specs=[pl.BlockSpec((tm, d), lambda i: (i, 0)),
                  pl.BlockSpec((tm, wa), lambda i: (i, 0)),
                  pl.BlockSpec((tm, wb), lambda i: (i, 0)),
                  pl.BlockSpec((tm, d), lambda i: (i, 0))]
        + gate_specs
        + [pl.BlockSpec((1, 2 * d), lambda i: (0, 0)),
           pl.BlockSpec((wa, d), lambda i: (0, 0), **resident),
           pl.BlockSpec((wb, d), lambda i: (0, 0), **resident),
           pl.BlockSpec((d, d), lambda i: (0, 0), **resident),
           pl.BlockSpec((1, d), lambda i: (0, 0))],
        out_specs=[pl.BlockSpec((tm, d), lambda i: (i, 0)),
                   pl.BlockSpec((tm, d), lambda i: (i, 0))],
        out_shape=[jax.ShapeDtypeStruct((s, d), F32),
                   jax.ShapeDtypeStruct((s, d), BF16)],
        compiler_params=_params("arbitrary"),
        name="gated_merge",
    )(xn, a, b, x, *gate_tiles, b_gate.reshape(1, 2 * d), w_a, w_b, w_o, g_norm.reshape(1, d))


def _pick(n, pref):
    return pref if n % pref == 0 else n


def kernel(x, norm_mix, w_in, b_gate, ln_v_g, ln_v_b, w_s, b_s, w_proj_a, w_proj_b,
           w_out, norm_ffn, w_gate_up, w_down, norm_final):
    batch, seq, d = x.shape
    depth = w_in.shape[0]
    sgu_w = SGU_GROUPS * SGU_GROUP_DIM
    sb_w = SB_HEADS * SB_HEAD_DIM
    assert sgu_w == sb_w
    off_q = 2 * sgu_w
    off_g = off_q + 3 * sb_w
    outs = []
    for b in range(batch):
        h = x.reshape(seq, d) if batch == 1 else x[b]
        for l in range(depth):
            w_first = w_in[l][:, :off_g].astype(BF16)
            xn, a, qkv = _sgu(h, norm_mix[l], w_first, ln_v_g[l], ln_v_b[l], w_s[l], b_s[l],
                              _pick(seq, SGU_ROWS))
            tn = _pick(d, GATE_COLS)
            whole = lambda wt: (wt, wt.shape[1], 0)
            gate_cols = [(w_in[l], tn, off_g // tn + k) for k in range(2 * d // tn)]
            o_b, bf = _attention(qkv, [whole(w_proj_a[l]), whole(w_proj_b[l]), whole(w_out[l]),
                                       whole(w_gate_up[l]), whole(w_down[l])] + gate_cols)
            wa_l, wb_l, wo_l, wgu_l, wd_l = bf[:5]
            h, hn = _gated_merge(xn, a, o_b, h, bf[5:], b_gate[l], wa_l, wb_l, wo_l,
                                 norm_ffn[l], _pick(seq, MERGE_ROWS))
            h = _ffn(hn, h, wgu_l, wd_l, norm_final, _pick(seq, FFN_UP_ROWS),
                     _pick(w_down.shape[1], FFN_COLS), _pick(seq, FFN_DOWN_ROWS),
                     final_norm=(l == depth - 1))
        outs.append(h)
    return outs[0].reshape(x.shape) if batch == 1 else jnp.stack(outs)
```

```python
import functools
import math

import jax
import jax.numpy as jnp
from jax import lax
from jax.experimental import pallas as pl
from jax.experimental.pallas import tpu as pltpu

EPS = 1e-6
CHUNK = 128
SGU_GROUPS = 8
SGU_GROUP_DIM = 128
SB_HEADS = 8
SB_HEAD_DIM = 128
ATT_BLOCK = 256
SGU_ROWS = 512
MERGE_ROWS = 256
GATE_COLS = 1024
FFN_UP_ROWS = 2048
FFN_UP_GROUP_ROWS = 1024
FFN_COLS = 512
FFN_DOWN_ROWS = 512
FFN_DOWN_GROUP_ROWS = 256
BF16_SUBLANES = 16
STICK_USED_F32_ZERO = 150.0
MASKED_SCORE = -1e30
SOFTPLUS_LINEAR_ABOVE = 90.0
LOG2_E = math.log2(math.e)
VMEM_LIMIT_BYTES = 60 * 1024 * 1024

F32 = jnp.float32
BF16 = jnp.bfloat16


def _params(*sem):
    return pltpu.CompilerParams(dimension_semantics=sem, vmem_limit_bytes=VMEM_LIMIT_BYTES)


def _softplus_bits(z):
    return jnp.where(z > SOFTPLUS_LINEAR_ABOVE, z, jnp.log(1.0 + jnp.exp2(z)) * LOG2_E)


def _gelu(x):
    return 0.5 * x * (1.0 + lax.erf(x * math.sqrt(0.5)))


def _sgu_kernel(x_ref, gn_ref, w_ref, wq_ref, wk_ref, wv_ref, lng_ref, lnb_ref, ws_ref, bs_ref,
                xn_ref, o_ref, qkv_ref):
    width = o_ref.shape[1]
    x = x_ref[...]
    ms = jnp.mean(x * x, axis=-1, keepdims=True)
    xn = (x * lax.rsqrt(ms + EPS) * gn_ref[...]).astype(BF16)
    xn_ref[...] = xn
    v = _gelu(jnp.dot(xn, w_ref[:, width:], preferred_element_type=F32))
    qkv_ref[:, :width] = jnp.dot(xn, wq_ref[...], preferred_element_type=F32).astype(BF16)
    mu = jnp.mean(v, axis=-1, keepdims=True)
    cen = v - mu
    var = jnp.mean(cen * cen, axis=-1, keepdims=True)
    vln = (cen * lax.rsqrt(var + EPS) * lng_ref[...] + lnb_ref[...]).astype(BF16)
    u = _gelu(jnp.dot(xn, w_ref[:, :width], preferred_element_type=F32))
    qkv_ref[:, width:2 * width] = jnp.dot(xn, wk_ref[...], preferred_element_type=F32).astype(BF16)
    qkv_ref[:, 2 * width:] = jnp.dot(xn, wv_ref[...], preferred_element_type=F32).astype(BF16)
    t_idx = lax.broadcasted_iota(jnp.int32, (CHUNK, CHUNK), 0)
    s_idx = lax.broadcasted_iota(jnp.int32, (CHUNK, CHUNK), 1)
    causal = s_idx <= t_idx
    for g in range(SGU_GROUPS):
        wm = jnp.where(causal, ws_ref[g], 0.0).astype(BF16)
        cols = slice(g * SGU_GROUP_DIM, (g + 1) * SGU_GROUP_DIM)
        for c in range(o_ref.shape[0] // CHUNK):
            rows = slice(c * CHUNK, (c + 1) * CHUNK)
            mixed = jnp.dot(wm, vln[rows, cols], preferred_element_type=F32) + bs_ref[g]
            o_ref[rows, cols] = (u[rows, cols] * mixed).astype(o_ref.dtype)


def _sgu(x, g_norm, w_in, ln_g, ln_b, w_s, b_s, tm):
    s, d = x.shape
    width = SGU_GROUPS * SGU_GROUP_DIM
    resident = dict(pipeline_mode=pl.Buffered(1))
    return pl.pallas_call(
        _sgu_kernel,
        grid=(s // tm,),
        in_specs=[pl.BlockSpec((tm, d), lambda i: (i, 0)),
                  pl.BlockSpec((1, d), lambda i: (0, 0)),
                  pl.BlockSpec((d, 2 * width), lambda i: (0, 0), **resident),
                  pl.BlockSpec((d, width), lambda i: (0, 2), **resident),
                  pl.BlockSpec((d, width), lambda i: (0, 3), **resident),
                  pl.BlockSpec((d, width), lambda i: (0, 4), **resident),
                  pl.BlockSpec((1, width), lambda i: (0, 0)),
                  pl.BlockSpec((1, width), lambda i: (0, 0)),
                  pl.BlockSpec(w_s.shape, lambda i: (0, 0, 0)),
                  pl.BlockSpec((SGU_GROUPS, CHUNK, 1), lambda i: (0, 0, 0))],
        out_specs=[pl.BlockSpec((tm, d), lambda i: (i, 0)),
                   pl.BlockSpec((tm, width), lambda i: (i, 0)),
                   pl.BlockSpec((tm, 3 * width), lambda i: (i, 0))],
        out_shape=[jax.ShapeDtypeStruct((s, d), BF16),
                   jax.ShapeDtypeStruct((s, width), BF16),
                   jax.ShapeDtypeStruct((s, 3 * width), BF16)],
        compiler_params=_params("arbitrary"),
        name="sgu_qkv",
    )(x, g_norm.reshape(1, d), w_in, w_in, w_in, w_in, ln_g.reshape(1, width),
      ln_b.reshape(1, width), w_s, b_s.reshape(SGU_GROUPS, CHUNK, 1))


def _attn_kernel(*refs, cast_every):
    n_cast = len(cast_every)
    q_ref, kc_ref, kp_ref, vc_ref, vp_ref, qkv_hbm = refs[:6]
    cast_in = refs[6:6 + n_cast]
    o_ref = refs[6 + n_cast]
    cast_out = refs[7 + n_cast:7 + 2 * n_cast]
    acc_ref, used_ref, kbuf, vbuf, sems = refs[7 + 2 * n_cast:]
    t = kp_ref.shape[0]
    dh = SB_HEAD_DIM
    width = q_ref.shape[1]
    heads = width // dh
    pair = pl.program_id(0)
    scale = LOG2_E / math.sqrt(dh)
    row = lax.broadcasted_iota(jnp.int32, (t, t), 0)
    col = lax.broadcasted_iota(jnp.int32, (t, t), 1)
    later = (row > col).astype(BF16)
    causal = col < row
    no_prev = jnp.where(pair > 0, 0.0, -MASKED_SCORE)
    first, second = slice(0, t), slice(t, 2 * t)

    for src, dst, every in zip(cast_in, cast_out, cast_every):
        if every == 1:
            dst[...] = src[...].astype(BF16)
        else:
            @pl.when(pair % every == 0)
            def _(src=src, dst=dst):
                dst[...] = src[...].astype(BF16)

    def scores(q, k):
        return lax.dot_general(q, k, (((1,), (1,)), ((), ())), preferred_element_type=F32) * scale

    def log_weights(z):
        used = _softplus_bits(z)
        logw = z - used - jnp.dot(used.astype(BF16), later, preferred_element_type=F32)
        return logw, jnp.sum(used, axis=-1, keepdims=True)

    least = [None, None]
    for h in range(heads):
        cols = slice(h * dh, (h + 1) * dh)
        z_a = scores(q_ref[:, cols], kc_ref[first, cols])
        z_ap = scores(q_ref[first, cols], kp_ref[:, cols])
        z_bb = scores(q_ref[second, cols], kc_ref[second, cols])
        logw, used_rows = log_weights(jnp.concatenate(
            [jnp.where(causal, z_a[first], MASKED_SCORE), z_ap,
             jnp.where(causal, z_bb, MASKED_SCORE), z_a[second]], axis=0))
        values = ((vc_ref[first, cols], vp_ref[:, cols]), (vc_ref[second, cols], vc_ref[first, cols]))
        for n, (rows, (v_diag, v_prev)) in enumerate(zip((first, second), values)):
            rows_d = slice(2 * n * t, (2 * n + 1) * t)
            rows_p = slice((2 * n + 1) * t, (2 * n + 2) * t)
            used_d, used_p = used_rows[rows_d], used_rows[rows_p]
            before = used_d + no_prev if n == 0 else used_d
            w = jnp.concatenate([jnp.exp2(logw[rows_d]), jnp.exp2(logw[rows_p] - before)],
                                axis=1).astype(BF16)
            acc_ref[rows, cols] = jnp.dot(w, jnp.concatenate([v_diag, v_prev], axis=0),
                                          preferred_element_type=F32)
            used = used_d + used_p
            used_ref[h, rows] = used
            least[n] = used if least[n] is None else jnp.minimum(least[n], used)

    def walk(rows, first_block, low):
        def cond(state):
            kb, low = state
            return jnp.logical_and(kb >= 0, low <= STICK_USED_F32_ZERO)

        def body(state):
            kb, _ = state
            src_rows = pl.ds(pl.multiple_of(kb * t, t), t)
            k_copy = pltpu.make_async_copy(qkv_hbm.at[src_rows, pl.ds(width, width)], kbuf,
                                           sems.at[0])
            v_copy = pltpu.make_async_copy(qkv_hbm.at[src_rows, pl.ds(2 * width, width)], vbuf,
                                           sems.at[1])
            k_copy.start()
            v_copy.start()
            k_copy.wait()
            v_copy.wait()
            least = None
            for h in range(heads):
                cols = slice(h * dh, (h + 1) * dh)
                logw, used_b = log_weights(scores(q_ref[rows, cols], kbuf[:, cols]))
                used = used_ref[h, rows]
                w = jnp.exp2(logw - used).astype(BF16)
                acc_ref[rows, cols] += jnp.dot(w, vbuf[:, cols], preferred_element_type=F32)
                used = used + used_b
                used_ref[h, rows] = used
                least = used if least is None else jnp.minimum(least, used)
            return kb - 1, jnp.min(least)

        lax.while_loop(cond, body, (first_block, low))

    walk(first, 2 * pair - 2, jnp.min(least[0]))
    walk(second, 2 * pair - 1, jnp.min(least[1]))
    o_ref[...] = acc_ref[...].astype(o_ref.dtype)


def _cast_every(rows, steps):
    every = 1
    while (rows * every) % (steps * BF16_SUBLANES) and every < steps:
        every *= 2
    return every


def _attention(qkv, weights):
    s = qkv.shape[0]
    t = ATT_BLOCK
    w = SB_HEADS * SB_HEAD_DIM
    steps = s // (2 * t)
    before = lambda i: jnp.maximum(2 * i - 1, 0)
    cast_every = tuple(_cast_every(wt.shape[0], steps) for wt, _, _ in weights)
    assert all(steps % e == 0 for e in cast_every)
    cast_in_specs = [pl.BlockSpec((wt.shape[0] * e // steps, nc), lambda i, e=e, cb=cb: (i // e, cb))
                     for (wt, nc, cb), e in zip(weights, cast_every)]
    cast_out_specs = [pl.BlockSpec((wt.shape[0] * e // steps, nc), lambda i, e=e: (i // e, 0))
                      for (wt, nc, _), e in zip(weights, cast_every)]
    outs = pl.pallas_call(
        functools.partial(_attn_kernel, cast_every=cast_every),
        grid=(steps,),
        in_specs=[pl.BlockSpec((2 * t, w), lambda i: (i, 0)),
                  pl.BlockSpec((2 * t, w), lambda i: (i, 1)),
                  pl.BlockSpec((t, w), lambda i: (before(i), 1)),
                  pl.BlockSpec((2 * t, w), lambda i: (i, 2)),
                  pl.BlockSpec((t, w), lambda i: (before(i), 2)),
                  pl.BlockSpec(memory_space=pl.ANY)] + cast_in_specs,
        out_specs=[pl.BlockSpec((2 * t, w), lambda i: (i, 0))] + cast_out_specs,
        out_shape=[jax.ShapeDtypeStruct((s, w), BF16)]
        + [jax.ShapeDtypeStruct((wt.shape[0], nc), BF16) for wt, nc, _ in weights],
        scratch_shapes=[pltpu.VMEM((2 * t, w), F32), pltpu.VMEM((SB_HEADS, 2 * t, 1), F32),
                        pltpu.VMEM((t, w), BF16), pltpu.VMEM((t, w), BF16),
                        pltpu.SemaphoreType.DMA((2,))],
        compiler_params=_params("arbitrary"),
        name="sb_attention",
    )(qkv, qkv, qkv, qkv, qkv, qkv, *[wt for wt, _, _ in weights])
    return outs[0], outs[1:]


def _ffn_up_kernel(hn_ref, wg_ref, wu_ref, h_ref):
    rows_per_group = _pick(hn_ref.shape[0], FFN_UP_GROUP_ROWS)
    for r in range(0, hn_ref.shape[0], rows_per_group):
        rows = slice(r, r + rows_per_group)
        hn = hn_ref[rows, :]
        gate = jnp.dot(hn, wg_ref[...], preferred_element_type=F32)
        up = jnp.dot(hn, wu_ref[...], preferred_element_type=F32)
        h_ref[rows, :] = (gate * jax.nn.sigmoid(gate) * up).astype(BF16)


def _ffn_down_kernel(h_ref, x_ref, wd_ref, gf_ref, o_ref, *, final_norm):
    rows_per_group = _pick(x_ref.shape[0], FFN_DOWN_GROUP_ROWS)
    for r in range(0, x_ref.shape[0], rows_per_group):
        rows = slice(r, r + rows_per_group)
        y = x_ref[rows, :] + jnp.dot(h_ref[rows, :], wd_ref[...], preferred_element_type=F32)
        if final_norm:
            ms = jnp.mean(y * y, axis=-1, keepdims=True)
            y = y * lax.rsqrt(ms + EPS) * gf_ref[...]
        o_ref[rows, :] = y


def _ffn(hn, x, w_gate_up, w_down, g_final, tm_up, tf, tm_down, final_norm):
    s, d = x.shape
    d_ff = w_down.shape[0]
    nf = d_ff // tf
    h = pl.pallas_call(
        _ffn_up_kernel,
        grid=(s // tm_up, nf),
        in_specs=[pl.BlockSpec((tm_up, d), lambda i, f: (i, 0)),
                  pl.BlockSpec((d, tf), lambda i, f: (0, f)),
                  pl.BlockSpec((d, tf), lambda i, f: (0, nf + f))],
        out_specs=pl.BlockSpec((tm_up, tf), lambda i, f: (i, f)),
        out_shape=jax.ShapeDtypeStruct((s, d_ff), BF16),
        compiler_params=_params("arbitrary", "arbitrary"),
        name="ffn_up",
    )(hn, w_gate_up, w_gate_up)
    return pl.pallas_call(
        functools.partial(_ffn_down_kernel, final_norm=final_norm),
        grid=(s // tm_down,),
        in_specs=[pl.BlockSpec((tm_down, d_ff), lambda i: (i, 0)),
                  pl.BlockSpec((tm_down, d), lambda i: (i, 0)),
                  pl.BlockSpec((d_ff, d), lambda i: (0, 0), pipeline_mode=pl.Buffered(1)),
                  pl.BlockSpec((1, d), lambda i: (0, 0))],
        out_specs=pl.BlockSpec((tm_down, d), lambda i: (i, 0)),
        out_shape=jax.ShapeDtypeStruct((s, d), F32),
        compiler_params=_params("arbitrary"),
        name="ffn_down",
    )(h, x, w_down, g_final.reshape(1, d))


def _gated_merge_kernel(xn_ref, a_ref, b_ref, x_ref, *refs):
    o_ref, hn_ref = refs[-2:]
    bg_ref, wa_ref, wb_ref, wo_ref, gn_ref = refs[-7:-2]
    wg_refs = refs[:-7]
    nc = len(wg_refs) // 2
    d = x_ref.shape[1]
    tn = d // nc
    xn = xn_ref[...]
    a = a_ref[...]
    b = b_ref[...]
    parts = []
    for c in range(nc):
        cols = slice(c * tn, (c + 1) * tn)
        g_a = jax.nn.sigmoid(jnp.dot(xn, wg_refs[c][...], preferred_element_type=F32)
                             + bg_ref[:, cols])
        g_b = jax.nn.sigmoid(jnp.dot(xn, wg_refs[nc + c][...], preferred_element_type=F32)
                             + bg_ref[:, d + c * tn:d + (c + 1) * tn])
        y_a = jnp.dot(a, wa_ref[:, cols], preferred_element_type=F32)
        y_b = jnp.dot(b, wb_ref[:, cols], preferred_element_type=F32)
        parts.append((g_a * y_a + g_b * y_b).astype(BF16))
    m = jnp.concatenate(parts, axis=1)
    y = x_ref[...] + jnp.dot(m, wo_ref[...], preferred_element_type=F32)
    o_ref[...] = y
    ms = jnp.mean(y * y, axis=-1, keepdims=True)
    hn_ref[...] = (y * lax.rsqrt(ms + EPS) * gn_ref[...]).astype(BF16)


def _gated_merge(xn, a, b, x, gate_tiles, b_gate, w_a, w_b, w_o, g_norm, tm):
    s, d = x.shape
    wa = a.shape[1]
    wb = b.shape[1]
    resident = dict(pipeline_mode=pl.Buffered(1))
    gate_specs = [pl.BlockSpec(g.shape, lambda i: (0, 0), **resident) for g in gate_tiles]
    return pl.pallas_call(
        _gated_merge_kernel,
        grid=(s // tm,),
        in_specs=[pl.BlockSpec((tm, d), lambda i: (i, 0)),
                  pl.BlockSpec((tm, wa), lambda i: (i, 0)),
                  pl.BlockSpec((tm, wb), lambda i: (i, 0)),
                  pl.BlockSpec((tm, d), lambda i: (i, 0))]
        + gate_specs
        + [pl.BlockSpec((1, 2 * d), lambda i: (0, 0)),
           pl.BlockSpec((wa, d), lambda i: (0, 0), **resident),
           pl.BlockSpec((wb, d), lambda i: (0, 0), **resident),
           pl.BlockSpec((d, d), lambda i: (0, 0), **resident),
           pl.BlockSpec((1, d), lambda i: (0, 0))],
        out_specs=[pl.BlockSpec((tm, d), lambda i: (i, 0)),
                   pl.BlockSpec((tm, d), lambda i: (i, 0))],
        out_shape=[jax.ShapeDtypeStruct((s, d), F32),
                   jax.ShapeDtypeStruct((s, d), BF16)],
        compiler_params=_params("arbitrary"),
        name="gated_merge",
    )(xn, a, b, x, *gate_tiles, b_gate.reshape(1, 2 * d), w_a, w_b, w_o, g_norm.reshape(1, d))


def _pick(n, pref):
    return pref if n % pref == 0 else n


def kernel(x, norm_mix, w_in, b_gate, ln_v_g, ln_v_b, w_s, b_s, w_proj_a, w_proj_b,
           w_out, norm_ffn, w_gate_up, w_down, norm_final):
    batch, seq, d = x.shape
    depth = w_in.shape[0]
    sgu_w = SGU_GROUPS * SGU_GROUP_DIM
    sb_w = SB_HEADS * SB_HEAD_DIM
    assert sgu_w == sb_w
    off_q = 2 * sgu_w
    off_g = off_q + 3 * sb_w
    outs = []
    for b in range(batch):
        h = x.reshape(seq, d) if batch == 1 else x[b]
        for l in range(depth):
            w_first = w_in[l][:, :off_g].astype(BF16)
            xn, a, qkv = _sgu(h, norm_mix[l], w_first, ln_v_g[l], ln_v_b[l], w_s[l], b_s[l],
                              _pick(seq, SGU_ROWS))
            tn = _pick(d, GATE_COLS)
            whole = lambda wt: (wt, wt.shape[1], 0)
            gate_cols = [(w_in[l], tn, off_g // tn + k) for k in range(2 * d // tn)]
            o_b, bf = _attention(qkv, [whole(w_proj_a[l]), whole(w_proj_b[l]), whole(w_out[l]),
                                       whole(w_gate_up[l]), whole(w_down[l])] + gate_cols)
            wa_l, wb_l, wo_l, wgu_l, wd_l = bf[:5]
            h, hn = _gated_merge(xn, a, o_b, h, bf[5:], b_gate[l], wa_l, wb_l, wo_l,
                                 norm_ffn[l], _pick(seq, MERGE_ROWS))
            h = _ffn(hn, h, wgu_l, wd_l, norm_final, _pick(seq, FFN_UP_ROWS),
                     _pick(w_down.shape[1], FFN_COLS), _pick(seq, FFN_DOWN_ROWS),
                     final_norm=(l == depth - 1))
        outs.append(h)
    return outs[0].reshape(x.shape) if batch == 1 else jnp.stack(outs)
```

```python
import functools
import math

import jax
import jax.numpy as jnp
from jax import lax
from jax.experimental import pallas as pl
from jax.experimental.pallas import tpu as pltpu

EPS = 1e-6
CHUNK = 128
SGU_GROUPS = 8
SGU_GROUP_DIM = 128
SB_HEADS = 8
SB_HEAD_DIM = 128
ATT_BLOCK = 256
SGU_ROWS = 512
MERGE_ROWS = 256
GATE_COLS = 1024
FFN_UP_ROWS = 4096
FFN_UP_GROUP_ROWS = 1024
FFN_COLS = 512
FFN_DOWN_ROWS = 512
FFN_DOWN_GROUP_ROWS = 256
BF16_SUBLANES = 16
STICK_USED_F32_ZERO = 150.0
MASKED_SCORE = -1e30
SOFTPLUS_LINEAR_ABOVE = 90.0
LOG2_E = math.log2(math.e)
VMEM_LIMIT_BYTES = 60 * 1024 * 1024

F32 = jnp.float32
BF16 = jnp.bfloat16


def _params(*sem):
    return pltpu.CompilerParams(dimension_semantics=sem, vmem_limit_bytes=VMEM_LIMIT_BYTES)


def _softplus_bits(z):
    return jnp.where(z > SOFTPLUS_LINEAR_ABOVE, z, jnp.log(1.0 + jnp.exp2(z)) * LOG2_E)


def _gelu(x):
    return 0.5 * x * (1.0 + lax.erf(x * math.sqrt(0.5)))


def _sgu_kernel(x_ref, gn_ref, w_ref, wq_ref, wk_ref, wv_ref, lng_ref, lnb_ref, ws_ref, bs_ref,
                xn_ref, o_ref, qkv_ref):
    width = o_ref.shape[1]
    x = x_ref[...]
    ms = jnp.mean(x * x, axis=-1, keepdims=True)
    xn = (x * lax.rsqrt(ms + EPS) * gn_ref[...]).astype(BF16)
    xn_ref[...] = xn
    v = _gelu(jnp.dot(xn, w_ref[:, width:], preferred_element_type=F32))
    qkv_ref[:, :width] = jnp.dot(xn, wq_ref[...], preferred_element_type=F32).astype(BF16)
    mu = jnp.mean(v, axis=-1, keepdims=True)
    cen = v - mu
    var = jnp.mean(cen * cen, axis=-1, keepdims=True)
    vln = (cen * lax.rsqrt(var + EPS) * lng_ref[...] + lnb_ref[...]).astype(BF16)
    u = _gelu(jnp.dot(xn, w_ref[:, :width], preferred_element_type=F32))
    qkv_ref[:, width:2 * width] = jnp.dot(xn, wk_ref[...], preferred_element_type=F32).astype(BF16)
    qkv_ref[:, 2 * width:] = jnp.dot(xn, wv_ref[...], preferred_element_type=F32).astype(BF16)
    t_idx = lax.broadcasted_iota(jnp.int32, (CHUNK, CHUNK), 0)
    s_idx = lax.broadcasted_iota(jnp.int32, (CHUNK, CHUNK), 1)
    causal = s_idx <= t_idx
    for g in range(SGU_GROUPS):
        wm = jnp.where(causal, ws_ref[g], 0.0).astype(BF16)
        cols = slice(g * SGU_GROUP_DIM, (g + 1) * SGU_GROUP_DIM)
        for c in range(o_ref.shape[0] // CHUNK):
            rows = slice(c * CHUNK, (c + 1) * CHUNK)
            mixed = jnp.dot(wm, vln[rows, cols], preferred_element_type=F32) + bs_ref[g]
            o_ref[rows, cols] = (u[rows, cols] * mixed).astype(o_ref.dtype)


def _sgu(x, g_norm, w_in, ln_g, ln_b, w_s, b_s, tm):
    s, d = x.shape
    width = SGU_GROUPS * SGU_GROUP_DIM
    resident = dict(pipeline_mode=pl.Buffered(1))
    return pl.pallas_call(
        _sgu_kernel,
        grid=(s // tm,),
        in_specs=[pl.BlockSpec((tm, d), lambda i: (i, 0)),
                  pl.BlockSpec((1, d), lambda i: (0, 0)),
                  pl.BlockSpec((d, 2 * width), lambda i: (0, 0), **resident),
                  pl.BlockSpec((d, width), lambda i: (0, 2), **resident),
                  pl.BlockSpec((d, width), lambda i: (0, 3), **resident),
                  pl.BlockSpec((d, width), lambda i: (0, 4), **resident),
                  pl.BlockSpec((1, width), lambda i: (0, 0)),
                  pl.BlockSpec((1, width), lambda i: (0, 0)),
                  pl.BlockSpec(w_s.shape, lambda i: (0, 0, 0)),
                  pl.BlockSpec((SGU_GROUPS, CHUNK, 1), lambda i: (0, 0, 0))],
        out_specs=[pl.BlockSpec((tm, d), lambda i: (i, 0)),
                   pl.BlockSpec((tm, width), lambda i: (i, 0)),
                   pl.BlockSpec((tm, 3 * width), lambda i: (i, 0))],
        out_shape=[jax.ShapeDtypeStruct((s, d), BF16),
                   jax.ShapeDtypeStruct((s, width), BF16),
                   jax.ShapeDtypeStruct((s, 3 * width), BF16)],
        compiler_params=_params("arbitrary"),
        name="sgu_qkv",
    )(x, g_norm.reshape(1, d), w_in, w_in, w_in, w_in, ln_g.reshape(1, width),
      ln_b.reshape(1, width), w_s, b_s.reshape(SGU_GROUPS, CHUNK, 1))


def _attn_kernel(*refs, cast_every):
    n_cast = len(cast_every)
    q_ref, kc_ref, kp_ref, vc_ref, vp_ref, qkv_hbm = refs[:6]
    cast_in = refs[6:6 + n_cast]
    o_ref = refs[6 + n_cast]
    cast_out = refs[7 + n_cast:7 + 2 * n_cast]
    acc_ref, used_ref, kbuf, vbuf, sems = refs[7 + 2 * n_cast:]
    t = kp_ref.shape[0]
    dh = SB_HEAD_DIM
    width = q_ref.shape[1]
    heads = width // dh
    pair = pl.program_id(0)
    scale = LOG2_E / math.sqrt(dh)
    row = lax.broadcasted_iota(jnp.int32, (t, t), 0)
    col = lax.broadcasted_iota(jnp.int32, (t, t), 1)
    later = (row > col).astype(BF16)
    causal = col < row
    no_prev = jnp.where(pair > 0, 0.0, -MASKED_SCORE)
    first, second = slice(0, t), slice(t, 2 * t)

    for src, dst, every in zip(cast_in, cast_out, cast_every):
        if every == 1:
            dst[...] = src[...].astype(BF16)
        else:
            @pl.when(pair % every == 0)
            def _(src=src, dst=dst):
                dst[...] = src[...].astype(BF16)

    def scores(q, k):
        return lax.dot_general(q, k, (((1,), (1,)), ((), ())), preferred_element_type=F32) * scale

    def log_weights(z):
        used = _softplus_bits(z)
        logw = z - used - jnp.dot(used.astype(BF16), later, preferred_element_type=F32)
        return logw, jnp.sum(used, axis=-1, keepdims=True)

    least = [None, None]
    for h in range(heads):
        cols = slice(h * dh, (h + 1) * dh)
        z_a = scores(q_ref[:, cols], kc_ref[first, cols])
        z_ap = scores(q_ref[first, cols], kp_ref[:, cols])
        z_bb = scores(q_ref[second, cols], kc_ref[second, cols])
        logw, used_rows = log_weights(jnp.concatenate(
            [jnp.where(causal, z_a[first], MASKED_SCORE), z_ap,
             jnp.where(causal, z_bb, MASKED_SCORE), z_a[second]], axis=0))
        values = ((vc_ref[first, cols], vp_ref[:, cols]), (vc_ref[second, cols], vc_ref[first, cols]))
        for n, (rows, (v_diag, v_prev)) in enumerate(zip((first, second), values)):
            rows_d = slice(2 * n * t, (2 * n + 1) * t)
            rows_p = slice((2 * n + 1) * t, (2 * n + 2) * t)
            used_d, used_p = used_rows[rows_d], used_rows[rows_p]
            before = used_d + no_prev if n == 0 else used_d
            w = jnp.concatenate([jnp.exp2(logw[rows_d]), jnp.exp2(logw[rows_p] - before)],
                                axis=1).astype(BF16)
            acc_ref[rows, cols] = jnp.dot(w, jnp.concatenate([v_diag, v_prev], axis=0),
                                          preferred_element_type=F32)
            used = used_d + used_p
            used_ref[h, rows] = used
            least[n] = used if least[n] is None else jnp.minimum(least[n], used)

    def walk(rows, first_block, low):
        def cond(state):
            kb, low = state
            return jnp.logical_and(kb >= 0, low <= STICK_USED_F32_ZERO)

        def body(state):
            kb, _ = state
            src_rows = pl.ds(pl.multiple_of(kb * t, t), t)
            k_copy = pltpu.make_async_copy(qkv_hbm.at[src_rows, pl.ds(width, width)], kbuf,
                                           sems.at[0])
            v_copy = pltpu.make_async_copy(qkv_hbm.at[src_rows, pl.ds(2 * width, width)], vbuf,
                                           sems.at[1])
            k_copy.start()
            v_copy.start()
            k_copy.wait()
            v_copy.wait()
            least = None
            for h in range(heads):
                cols = slice(h * dh, (h + 1) * dh)
                logw, used_b = log_weights(scores(q_ref[rows, cols], kbuf[:, cols]))
                used = used_ref[h, rows]
                w = jnp.exp2(logw - used).astype(BF16)
                acc_ref[rows, cols] += jnp.dot(w, vbuf[:, cols], preferred_element_type=F32)
                used = used + used_b
                used_ref[h, rows] = used
                least = used if least is None else jnp.minimum(least, used)
            return kb - 1, jnp.min(least)

        lax.while_loop(cond, body, (first_block, low))

    walk(first, 2 * pair - 2, jnp.min(least[0]))
    walk(second, 2 * pair - 1, jnp.min(least[1]))
    o_ref[...] = acc_ref[...].astype(o_ref.dtype)


def _cast_every(rows, steps):
    every = 1
    while (rows * every) % (steps * BF16_SUBLANES) and every < steps:
        every *= 2
    return every


def _attention(qkv, weights):
    s = qkv.shape[0]
    t = ATT_BLOCK
    w = SB_HEADS * SB_HEAD_DIM
    steps = s // (2 * t)
    before = lambda i: jnp.maximum(2 * i - 1, 0)
    cast_every = tuple(_cast_every(wt.shape[0], steps) for wt, _, _ in weights)
    assert all(steps % e == 0 for e in cast_every)
    cast_in_specs = [pl.BlockSpec((wt.shape[0] * e // steps, nc), lambda i, e=e, cb=cb: (i // e, cb))
                     for (wt, nc, cb), e in zip(weights, cast_every)]
    cast_out_specs = [pl.BlockSpec((wt.shape[0] * e // steps, nc), lambda i, e=e: (i // e, 0))
                      for (wt, nc, _), e in zip(weights, cast_every)]
    outs = pl.pallas_call(
        functools.partial(_attn_kernel, cast_every=cast_every),
        grid=(steps,),
        in_specs=[pl.BlockSpec((2 * t, w), lambda i: (i, 0)),
                  pl.BlockSpec((2 * t, w), lambda i: (i, 1)),
                  pl.BlockSpec((t, w), lambda i: (before(i), 1)),
                  pl.BlockSpec((2 * t, w), lambda i: (i, 2)),
                  pl.BlockSpec((t, w), lambda i: (before(i), 2)),
                  pl.BlockSpec(memory_space=pl.ANY)] + cast_in_specs,
        out_specs=[pl.BlockSpec((2 * t, w), lambda i: (i, 0))] + cast_out_specs,
        out_shape=[jax.ShapeDtypeStruct((s, w), BF16)]
        + [jax.ShapeDtypeStruct((wt.shape[0], nc), BF16) for wt, nc, _ in weights],
        scratch_shapes=[pltpu.VMEM((2 * t, w), F32), pltpu.VMEM((SB_HEADS, 2 * t, 1), F32),
                        pltpu.VMEM((t, w), BF16), pltpu.VMEM((t, w), BF16),
                        pltpu.SemaphoreType.DMA((2,))],
        compiler_params=_params("arbitrary"),
        name="sb_attention",
    )(qkv, qkv, qkv, qkv, qkv, qkv, *[wt for wt, _, _ in weights])
    return outs[0], outs[1:]


def _ffn_up_kernel(hn_ref, wg_ref, wu_ref, h_ref):
    rows_per_group = _pick(hn_ref.shape[0], FFN_UP_GROUP_ROWS)
    for r in range(0, hn_ref.shape[0], rows_per_group):
        rows = slice(r, r + rows_per_group)
        hn = hn_ref[rows, :]
        gate = jnp.dot(hn, wg_ref[...], preferred_element_type=F32)
        up = jnp.dot(hn, wu_ref[...], preferred_element_type=F32)
        h_ref[rows, :] = (gate * jax.nn.sigmoid(gate) * up).astype(BF16)


def _ffn_down_kernel(h_ref, x_ref, wd_ref, gf_ref, o_ref, *, final_norm):
    rows_per_group = _pick(x_ref.shape[0], FFN_DOWN_GROUP_ROWS)
    for r in range(0, x_ref.shape[0], rows_per_group):
        rows = slice(r, r + rows_per_group)
        y = x_ref[rows, :] + jnp.dot(h_ref[rows, :], wd_ref[...], preferred_element_type=F32)
        if final_norm:
            ms = jnp.mean(y * y, axis=-1, keepdims=True)
            y = y * lax.rsqrt(ms + EPS) * gf_ref[...]
        o_ref[rows, :] = y


def _ffn(hn, x, w_gate_up, w_down, g_final, tm_up, tf, tm_down, final_norm):
    s, d = x.shape
    d_ff = w_down.shape[0]
    nf = d_ff // tf
    h = pl.pallas_call(
        _ffn_up_kernel,
        grid=(s // tm_up, nf),
        in_specs=[pl.BlockSpec((tm_up, d), lambda i, f: (i, 0)),
                  pl.BlockSpec((d, tf), lambda i, f: (0, f)),
                  pl.BlockSpec((d, tf), lambda i, f: (0, nf + f))],
        out_specs=pl.BlockSpec((tm_up, tf), lambda i, f: (i, f)),
        out_shape=jax.ShapeDtypeStruct((s, d_ff), BF16),
        compiler_params=_params("arbitrary", "arbitrary"),
        name="ffn_up",
    )(hn, w_gate_up, w_gate_up)
    return pl.pallas_call(
        functools.partial(_ffn_down_kernel, final_norm=final_norm),
        grid=(s // tm_down,),
        in_specs=[pl.BlockSpec((tm_down, d_ff), lambda i: (i, 0)),
                  pl.BlockSpec((tm_down, d), lambda i: (i, 0)),
                  pl.BlockSpec((d_ff, d), lambda i: (0, 0), pipeline_mode=pl.Buffered(1)),
                  pl.BlockSpec((1, d), lambda i: (0, 0))],
        out_specs=pl.BlockSpec((tm_down, d), lambda i: (i, 0)),
        out_shape=jax.ShapeDtypeStruct((s, d), F32),
        compiler_params=_params("arbitrary"),
        name="ffn_down",
    )(h, x, w_down, g_final.reshape(1, d))


def _gated_merge_kernel(xn_ref, a_ref, b_ref, x_ref, *refs):
    o_ref, hn_ref = refs[-2:]
    bg_ref, wa_ref, wb_ref, wo_ref, gn_ref = refs[-7:-2]
    wg_refs = refs[:-7]
    nc = len(wg_refs) // 2
    d = x_ref.shape[1]
    tn = d // nc
    xn = xn_ref[...]
    a = a_ref[...]
    b = b_ref[...]
    parts = []
    for c in range(nc):
        cols = slice(c * tn, (c + 1) * tn)
        g_a = jax.nn.sigmoid(jnp.dot(xn, wg_refs[c][...], preferred_element_type=F32)
                             + bg_ref[:, cols])
        g_b = jax.nn.sigmoid(jnp.dot(xn, wg_refs[nc + c][...], preferred_element_type=F32)
                             + bg_ref[:, d + c * tn:d + (c + 1) * tn])
        y_a = jnp.dot(a, wa_ref[:, cols], preferred_element_type=F32)
        y_b = jnp.dot(b, wb_ref[:, cols], preferred_element_type=F32)
        parts.append((g_a * y_a + g_b * y_b).astype(BF16))
    m = jnp.concatenate(parts, axis=1)
    y = x_ref[...] + jnp.dot(m, wo_ref[...], preferred_element_type=F32)
    o_ref[...] = y
    ms = jnp.mean(y * y, axis=-1, keepdims=True)
    hn_ref[...] = (y * lax.rsqrt(ms + EPS) * gn_ref[...]).astype(BF16)


def _gated_merge(xn, a, b, x, gate_tiles, b_gate, w_a, w_b, w_o, g_norm, tm):
    s, d = x.shape
    wa = a.shape[1]
    wb = b.shape[1]
    resident = dict(pipeline_mode=pl.Buffered(1))
    gate_specs = [pl.BlockSpec(g.shape, lambda i: (0, 0), **resident) for g in gate_tiles]
    return pl.pallas_call(
        _gated_merge_kernel,
        grid=(s // tm,),
        in_specs=[pl.BlockSpec((tm, d), lambda i: (i, 0)),
                  pl.BlockSpec((tm, wa), lambda i: (i, 0)),
                  pl.BlockSpec((tm, wb), lambda i: (i, 0)),
                  pl.BlockSpec((tm, d), lambda i: (i, 0))]
        + gate_specs
        + [pl.BlockSpec((1, 2 * d), lambda i: (0, 0)),
           pl.BlockSpec((wa, d), lambda i: (0, 0), **resident),
           pl.BlockSpec((wb, d), lambda i: (0, 0), **resident),
           pl.BlockSpec((d, d), lambda i: (0, 0), **resident),
           pl.BlockSpec((1, d), lambda i: (0, 0))],
        out_specs=[pl.BlockSpec((tm, d), lambda i: (i, 0)),
                   pl.BlockSpec((tm, d), lambda i: (i, 0))],
        out_shape=[jax.ShapeDtypeStruct((s, d), F32),
                   jax.ShapeDtypeStruct((s, d), BF16)],
        compiler_params=_params("arbitrary"),
        name="gated_merge",
    )(xn, a, b, x, *gate_tiles, b_gate.reshape(1, 2 * d), w_a, w_b, w_o, g_norm.reshape(1, d))


def _pick(n, pref):
    return pref if n % pref == 0 else n


def kernel(x, norm_mix, w_in, b_gate, ln_v_g, ln_v_b, w_s, b_s, w_proj_a, w_proj_b,
           w_out, norm_ffn, w_gate_up, w_down, norm_final):
    batch, seq, d = x.shape
    depth = w_in.shape[0]
    sgu_w = SGU_GROUPS * SGU_GROUP_DIM
    sb_w = SB_HEADS * SB_HEAD_DIM
    assert sgu_w == sb_w
    off_q = 2 * sgu_w
    off_g = off_q + 3 * sb_w
    outs = []
    for b in range(batch):
        h = x.reshape(seq, d) if batch == 1 else x[b]
        for l in range(depth):
            w_first = w_in[l][:, :off_g].astype(BF16)
            xn, a, qkv = _sgu(h, norm_mix[l], w_first, ln_v_g[l], ln_v_b[l], w_s[l], b_s[l],
                              _pick(seq, SGU_ROWS))
            tn = _pick(d, GATE_COLS)
            whole = lambda wt: (wt, wt.shape[1], 0)
            gate_cols = [(w_in[l], tn, off_g // tn + k) for k in range(2 * d // tn)]
            o_b, bf = _attention(qkv, [whole(w_proj_a[l]), whole(w_proj_b[l]), whole(w_out[l]),
                                       whole(w_gate_up[l]), whole(w_down[l])] + gate_cols)
            wa_l, wb_l, wo_l, wgu_l, wd_l = bf[:5]
            h, hn = _gated_merge(xn, a, o_b, h, bf[5:], b_gate[l], wa_l, wb_l, wo_l,
                                 norm_ffn[l], _pick(seq, MERGE_ROWS))
            h = _ffn(hn, h, wgu_l, wd_l, norm_final, _pick(seq, FFN_UP_ROWS),
                     _pick(w_down.shape[1], FFN_COLS), _pick(seq, FFN_DOWN_ROWS),
                     final_norm=(l == depth - 1))
        outs.append(h)
    return outs[0].reshape(x.shape) if batch == 1 else jnp.stack(outs)
```

```python
import functools
import math

import jax
import jax.numpy as jnp
from jax import lax
from jax.experimental import pallas as pl
from jax.experimental.pallas import tpu as pltpu

EPS = 1e-6
CHUNK = 128
SGU_GROUPS = 8
SGU_GROUP_DIM = 128
SB_HEADS = 8
SB_HEAD_DIM = 128
ATT_BLOCK = 256
SGU_ROWS = 512
MERGE_ROWS = 512
OUT_PROJ_ROWS = 1024
MERGE_GROUP_ROWS = 256
GATE_COLS = 1024
FFN_UP_ROWS = 4096
FFN_UP_GROUP_ROWS = 1024
FFN_COLS = 512
FFN_DOWN_ROWS = 512
FFN_DOWN_GROUP_ROWS = 256
BF16_SUBLANES = 16
STICK_USED_F32_ZERO = 150.0
MASKED_SCORE = -1e30
SOFTPLUS_LINEAR_ABOVE = 90.0
LOG2_E = math.log2(math.e)
VMEM_LIMIT_BYTES = 60 * 1024 * 1024

F32 = jnp.float32
BF16 = jnp.bfloat16


def _params(*sem):
    return pltpu.CompilerParams(dimension_semantics=sem, vmem_limit_bytes=VMEM_LIMIT_BYTES)


def _softplus_bits(z):
    return jnp.where(z > SOFTPLUS_LINEAR_ABOVE, z, jnp.log(1.0 + jnp.exp2(z)) * LOG2_E)


def _gelu(x):
    return 0.5 * x * (1.0 + lax.erf(x * math.sqrt(0.5)))


def _sgu_kernel(x_ref, gn_ref, w_ref, wq_ref, wk_ref, wv_ref, lng_ref, lnb_ref, ws_ref, bs_ref,
                xn_ref, o_ref, qkv_ref):
    width = o_ref.shape[1]
    x = x_ref[...]
    ms = jnp.mean(x * x, axis=-1, keepdims=True)
    xn = (x * lax.rsqrt(ms + EPS) * gn_ref[...]).astype(BF16)
    xn_ref[...] = xn
    v = _gelu(jnp.dot(xn, w_ref[:, width:], preferred_element_type=F32))
    qkv_ref[:, :width] = jnp.dot(xn, wq_ref[...], preferred_element_type=F32).astype(BF16)
    mu = jnp.mean(v, axis=-1, keepdims=True)
    cen = v - mu
    var = jnp.mean(cen * cen, axis=-1, keepdims=True)
    vln = (cen * lax.rsqrt(var + EPS) * lng_ref[...] + lnb_ref[...]).astype(BF16)
    u = _gelu(jnp.dot(xn, w_ref[:, :width], preferred_element_type=F32))
    qkv_ref[:, width:2 * width] = jnp.dot(xn, wk_ref[...], preferred_element_type=F32).astype(BF16)
    qkv_ref[:, 2 * width:] = jnp.dot(xn, wv_ref[...], preferred_element_type=F32).astype(BF16)
    t_idx = lax.broadcasted_iota(jnp.int32, (CHUNK, CHUNK), 0)
    s_idx = lax.broadcasted_iota(jnp.int32, (CHUNK, CHUNK), 1)
    causal = s_idx <= t_idx
    for g in range(SGU_GROUPS):
        wm = jnp.where(causal, ws_ref[g], 0.0).astype(BF16)
        cols = slice(g * SGU_GROUP_DIM, (g + 1) * SGU_GROUP_DIM)
        for c in range(o_ref.shape[0] // CHUNK):
            rows = slice(c * CHUNK, (c + 1) * CHUNK)
            mixed = jnp.dot(wm, vln[rows, cols], preferred_element_type=F32) + bs_ref[g]
            o_ref[rows, cols] = (u[rows, cols] * mixed).astype(o_ref.dtype)


def _sgu(x, g_norm, w_in, ln_g, ln_b, w_s, b_s, tm):
    s, d = x.shape
    width = SGU_GROUPS * SGU_GROUP_DIM
    resident = dict(pipeline_mode=pl.Buffered(1))
    return pl.pallas_call(
        _sgu_kernel,
        grid=(s // tm,),
        in_specs=[pl.BlockSpec((tm, d), lambda i: (i, 0)),
                  pl.BlockSpec((1, d), lambda i: (0, 0)),
                  pl.BlockSpec((d, 2 * width), lambda i: (0, 0), **resident),
                  pl.BlockSpec((d, width), lambda i: (0, 2), **resident),
                  pl.BlockSpec((d, width), lambda i: (0, 3), **resident),
                  pl.BlockSpec((d, width), lambda i: (0, 4), **resident),
                  pl.BlockSpec((1, width), lambda i: (0, 0)),
                  pl.BlockSpec((1, width), lambda i: (0, 0)),
                  pl.BlockSpec(w_s.shape, lambda i: (0, 0, 0)),
                  pl.BlockSpec((SGU_GROUPS, CHUNK, 1), lambda i: (0, 0, 0))],
        out_specs=[pl.BlockSpec((tm, d), lambda i: (i, 0)),
                   pl.BlockSpec((tm, width), lambda i: (i, 0)),
                   pl.BlockSpec((tm, 3 * width), lambda i: (i, 0))],
        out_shape=[jax.ShapeDtypeStruct((s, d), BF16),
                   jax.ShapeDtypeStruct((s, width), BF16),
                   jax.ShapeDtypeStruct((s, 3 * width), BF16)],
        compiler_params=_params("arbitrary"),
        name="sgu_qkv",
    )(x, g_norm.reshape(1, d), w_in, w_in, w_in, w_in, ln_g.reshape(1, width),
      ln_b.reshape(1, width), w_s, b_s.reshape(SGU_GROUPS, CHUNK, 1))


def _attn_kernel(*refs, cast_every):
    n_cast = len(cast_every)
    q_ref, kc_ref, kp_ref, vc_ref, vp_ref, qkv_hbm = refs[:6]
    cast_in = refs[6:6 + n_cast]
    o_ref = refs[6 + n_cast]
    cast_out = refs[7 + n_cast:7 + 2 * n_cast]
    acc_ref, used_ref, kbuf, vbuf, sems = refs[7 + 2 * n_cast:]
    t = kp_ref.shape[0]
    dh = SB_HEAD_DIM
    width = q_ref.shape[1]
    heads = width // dh
    pair = pl.program_id(0)
    scale = LOG2_E / math.sqrt(dh)
    row = lax.broadcasted_iota(jnp.int32, (t, t), 0)
    col = lax.broadcasted_iota(jnp.int32, (t, t), 1)
    later = (row > col).astype(BF16)
    causal = col < row
    no_prev = jnp.where(pair > 0, 0.0, -MASKED_SCORE)
    first, second = slice(0, t), slice(t, 2 * t)

    for src, dst, every in zip(cast_in, cast_out, cast_every):
        if every == 1:
            dst[...] = src[...].astype(BF16)
        else:
            @pl.when(pair % every == 0)
            def _(src=src, dst=dst):
                dst[...] = src[...].astype(BF16)

    def scores(q, k):
        return lax.dot_general(q, k, (((1,), (1,)), ((), ())), preferred_element_type=F32) * scale

    def log_weights(z):
        used = _softplus_bits(z)
        logw = z - used - jnp.dot(used.astype(BF16), later, preferred_element_type=F32)
        return logw, jnp.sum(used, axis=-1, keepdims=True)

    least = [None, None]
    for h in range(heads):
        cols = slice(h * dh, (h + 1) * dh)
        z_a = scores(q_ref[:, cols], kc_ref[first, cols])
        z_ap = scores(q_ref[first, cols], kp_ref[:, cols])
        z_bb = scores(q_ref[second, cols], kc_ref[second, cols])
        logw, used_rows = log_weights(jnp.concatenate(
            [jnp.where(causal, z_a[first], MASKED_SCORE), z_ap,
             jnp.where(causal, z_bb, MASKED_SCORE), z_a[second]], axis=0))
        values = ((vc_ref[first, cols], vp_ref[:, cols]), (vc_ref[second, cols], vc_ref[first, cols]))
        for n, (rows, (v_diag, v_prev)) in enumerate(zip((first, second), values)):
            rows_d = slice(2 * n * t, (2 * n + 1) * t)
            rows_p = slice((2 * n + 1) * t, (2 * n + 2) * t)
            used_d, used_p = used_rows[rows_d], used_rows[rows_p]
            before = used_d + no_prev if n == 0 else used_d
            w = jnp.concatenate([jnp.exp2(logw[rows_d]), jnp.exp2(logw[rows_p] - before)],
                                axis=1).astype(BF16)
            acc_ref[rows, cols] = jnp.dot(w, jnp.concatenate([v_diag, v_prev], axis=0),
                                          preferred_element_type=F32)
            used = used_d + used_p
            used_ref[h, rows] = used
            least[n] = used if least[n] is None else jnp.minimum(least[n], used)

    def walk(rows, first_block, low):
        def cond(state):
            kb, low = state
            return jnp.logical_and(kb >= 0, low <= STICK_USED_F32_ZERO)

        def body(state):
            kb, _ = state
            src_rows = pl.ds(pl.multiple_of(kb * t, t), t)
            k_copy = pltpu.make_async_copy(qkv_hbm.at[src_rows, pl.ds(width, width)], kbuf,
                                           sems.at[0])
            v_copy = pltpu.make_async_copy(qkv_hbm.at[src_rows, pl.ds(2 * width, width)], vbuf,
                                           sems.at[1])
            k_copy.start()
            v_copy.start()
            k_copy.wait()
            v_copy.wait()
            least = None
            for h in range(heads):
                cols = slice(h * dh, (h + 1) * dh)
                logw, used_b = log_weights(scores(q_ref[rows, cols], kbuf[:, cols]))
                used = used_ref[h, rows]
                w = jnp.exp2(logw - used).astype(BF16)
                acc_ref[rows, cols] += jnp.dot(w, vbuf[:, cols], preferred_element_type=F32)
                used = used + used_b
                used_ref[h, rows] = used
                least = used if least is None else jnp.minimum(least, used)
            return kb - 1, jnp.min(least)

        lax.while_loop(cond, body, (first_block, low))

    walk(first, 2 * pair - 2, jnp.min(least[0]))
    walk(second, 2 * pair - 1, jnp.min(least[1]))
    o_ref[...] = acc_ref[...].astype(o_ref.dtype)


def _cast_every(rows, steps):
    every = 1
    while (rows * every) % (steps * BF16_SUBLANES) and every < steps:
        every *= 2
    return every


def _attention(qkv, weights):
    s = qkv.shape[0]
    t = ATT_BLOCK
    w = SB_HEADS * SB_HEAD_DIM
    steps = s // (2 * t)
    before = lambda i: jnp.maximum(2 * i - 1, 0)
    cast_every = tuple(_cast_every(wt.shape[0], steps) for wt, _, _ in weights)
    assert all(steps % e == 0 for e in cast_every)
    cast_in_specs = [pl.BlockSpec((wt.shape[0] * e // steps, nc), lambda i, e=e, cb=cb: (i // e, cb))
                     for (wt, nc, cb), e in zip(weights, cast_every)]
    cast_out_specs = [pl.BlockSpec((wt.shape[0] * e // steps, nc), lambda i, e=e: (i // e, 0))
                      for (wt, nc, _), e in zip(weights, cast_every)]
    outs = pl.pallas_call(
        functools.partial(_attn_kernel, cast_every=cast_every),
        grid=(steps,),
        in_specs=[pl.BlockSpec((2 * t, w), lambda i: (i, 0)),
                  pl.BlockSpec((2 * t, w), lambda i: (i, 1)),
                  pl.BlockSpec((t, w), lambda i: (before(i), 1)),
                  pl.BlockSpec((2 * t, w), lambda i: (i, 2)),
                  pl.BlockSpec((t, w), lambda i: (before(i), 2)),
                  pl.BlockSpec(memory_space=pl.ANY)] + cast_in_specs,
        out_specs=[pl.BlockSpec((2 * t, w), lambda i: (i, 0))] + cast_out_specs,
        out_shape=[jax.ShapeDtypeStruct((s, w), BF16)]
        + [jax.ShapeDtypeStruct((wt.shape[0], nc), BF16) for wt, nc, _ in weights],
        scratch_shapes=[pltpu.VMEM((2 * t, w), F32), pltpu.VMEM((SB_HEADS, 2 * t, 1), F32),
                        pltpu.VMEM((t, w), BF16), pltpu.VMEM((t, w), BF16),
                        pltpu.SemaphoreType.DMA((2,))],
        compiler_params=_params("arbitrary"),
        name="sb_attention",
    )(qkv, qkv, qkv, qkv, qkv, qkv, *[wt for wt, _, _ in weights])
    return outs[0], outs[1:]


def _ffn_up_kernel(hn_ref, wg_ref, wu_ref, h_ref):
    rows_per_group = _pick(hn_ref.shape[0], FFN_UP_GROUP_ROWS)
    for r in range(0, hn_ref.shape[0], rows_per_group):
        rows = slice(r, r + rows_per_group)
        hn = hn_ref[rows, :]
        gate = jnp.dot(hn, wg_ref[...], preferred_element_type=F32)
        up = jnp.dot(hn, wu_ref[...], preferred_element_type=F32)
        h_ref[rows, :] = (gate * jax.nn.sigmoid(gate) * up).astype(BF16)


def _ffn_down_kernel(h_ref, x_ref, wd_ref, gf_ref, o_ref, *, final_norm):
    rows_per_group = _pick(x_ref.shape[0], FFN_DOWN_GROUP_ROWS)
    for r in range(0, x_ref.shape[0], rows_per_group):
        rows = slice(r, r + rows_per_group)
        y = x_ref[rows, :] + jnp.dot(h_ref[rows, :], wd_ref[...], preferred_element_type=F32)
        if final_norm:
            ms = jnp.mean(y * y, axis=-1, keepdims=True)
            y = y * lax.rsqrt(ms + EPS) * gf_ref[...]
        o_ref[rows, :] = y


def _ffn(hn, x, w_gate_up, w_down, g_final, tm_up, tf, tm_down, final_norm):
    s, d = x.shape
    d_ff = w_down.shape[0]
    nf = d_ff // tf
    h = pl.pallas_call(
        _ffn_up_kernel,
        grid=(s // tm_up, nf),
        in_specs=[pl.BlockSpec((tm_up, d), lambda i, f: (i, 0)),
                  pl.BlockSpec((d, tf), lambda i, f: (0, f)),
                  pl.BlockSpec((d, tf), lambda i, f: (0, nf + f))],
        out_specs=pl.BlockSpec((tm_up, tf), lambda i, f: (i, f)),
        out_shape=jax.ShapeDtypeStruct((s, d_ff), BF16),
        compiler_params=_params("arbitrary", "arbitrary"),
        name="ffn_up",
    )(hn, w_gate_up, w_gate_up)
    return pl.pallas_call(
        functools.partial(_ffn_down_kernel, final_norm=final_norm),
        grid=(s // tm_down,),
        in_specs=[pl.BlockSpec((tm_down, d_ff), lambda i: (i, 0)),
                  pl.BlockSpec((tm_down, d), lambda i: (i, 0)),
                  pl.BlockSpec((d_ff, d), lambda i: (0, 0), pipeline_mode=pl.Buffered(1)),
                  pl.BlockSpec((1, d), lambda i: (0, 0))],
        out_specs=pl.BlockSpec((tm_down, d), lambda i: (i, 0)),
        out_shape=jax.ShapeDtypeStruct((s, d), F32),
        compiler_params=_params("arbitrary"),
        name="ffn_down",
    )(h, x, w_down, g_final.reshape(1, d))


def _gated_branches_kernel(xn_ref, a_ref, b_ref, *refs):
    m_ref = refs[-1]
    bg_ref, wa_ref, wb_ref = refs[-4:-1]
    wg_refs = refs[:-4]
    nc = len(wg_refs) // 2
    d = m_ref.shape[1]
    tn = d // nc
    rows_per_group = _pick(m_ref.shape[0], MERGE_GROUP_ROWS)
    for r in range(0, m_ref.shape[0], rows_per_group):
        rows = slice(r, r + rows_per_group)
        xn = xn_ref[rows, :]
        a = a_ref[rows, :]
        b = b_ref[rows, :]
        for c in range(nc):
            cols = slice(c * tn, (c + 1) * tn)
            g_a = jax.nn.sigmoid(jnp.dot(xn, wg_refs[c][...], preferred_element_type=F32)
                                 + bg_ref[:, cols])
            g_b = jax.nn.sigmoid(jnp.dot(xn, wg_refs[nc + c][...], preferred_element_type=F32)
                                 + bg_ref[:, d + c * tn:d + (c + 1) * tn])
            y_a = jnp.dot(a, wa_ref[:, cols], preferred_element_type=F32)
            y_b = jnp.dot(b, wb_ref[:, cols], preferred_element_type=F32)
            m_ref[rows, cols] = (g_a * y_a + g_b * y_b).astype(BF16)


def _out_proj_kernel(m_ref, x_ref, wo_ref, gn_ref, o_ref, hn_ref):
    rows_per_group = _pick(x_ref.shape[0], MERGE_GROUP_ROWS)
    for r in range(0, x_ref.shape[0], rows_per_group):
        rows = slice(r, r + rows_per_group)
        y = x_ref[rows, :] + jnp.dot(m_ref[rows, :], wo_ref[...], preferred_element_type=F32)
        o_ref[rows, :] = y
        ms = jnp.mean(y * y, axis=-1, keepdims=True)
        hn_ref[rows, :] = (y * lax.rsqrt(ms + EPS) * gn_ref[...]).astype(BF16)


def _gated_merge(xn, a, b, x, gate_tiles, b_gate, w_a, w_b, w_o, g_norm, tm):
    s, d = x.shape
    wa = a.shape[1]
    wb = b.shape[1]
    resident = dict(pipeline_mode=pl.Buffered(1))
    gate_specs = [pl.BlockSpec(g.shape, lambda i: (0, 0), **resident) for g in gate_tiles]
    m = pl.pallas_call(
        _gated_branches_kernel,
        grid=(s // tm,),
        in_specs=[pl.BlockSpec((tm, d), lambda i: (i, 0)),
                  pl.BlockSpec((tm, wa), lambda i: (i, 0)),
                  pl.BlockSpec((tm, wb), lambda i: (i, 0))]
        + gate_specs
        + [pl.BlockSpec((1, 2 * d), lambda i: (0, 0)),
           pl.BlockSpec((wa, d), lambda i: (0, 0), **resident),
           pl.BlockSpec((wb, d), lambda i: (0, 0), **resident)],
        out_specs=pl.BlockSpec((tm, d), lambda i: (i, 0)),
        out_shape=jax.ShapeDtypeStruct((s, d), BF16),
        compiler_params=_params("arbitrary"),
        name="gated_branches",
    )(xn, a, b, *gate_tiles, b_gate.reshape(1, 2 * d), w_a, w_b)
    to = _pick(s, OUT_PROJ_ROWS)
    return pl.pallas_call(
        _out_proj_kernel,
        grid=(s // to,),
        in_specs=[pl.BlockSpec((to, d), lambda i: (i, 0)),
                  pl.BlockSpec((to, d), lambda i: (i, 0)),
                  pl.BlockSpec((d, d), lambda i: (0, 0), **resident),
                  pl.BlockSpec((1, d), lambda i: (0, 0))],
        out_specs=[pl.BlockSpec((to, d), lambda i: (i, 0)),
                   pl.BlockSpec((to, d), lambda i: (i, 0))],
        out_shape=[jax.ShapeDtypeStruct((s, d), F32),
                   jax.ShapeDtypeStruct((s, d), BF16)],
        compiler_params=_params("arbitrary"),
        name="out_proj",
    )(m, x, w_o, g_norm.reshape(1, d))


def _pick(n, pref):
    return pref if n % pref == 0 else n


def kernel(x, norm_mix, w_in, b_gate, ln_v_g, ln_v_b, w_s, b_s, w_proj_a, w_proj_b,
           w_out, norm_ffn, w_gate_up, w_down, norm_final):
    batch, seq, d = x.shape
    depth = w_in.shape[0]
    sgu_w = SGU_GROUPS * SGU_GROUP_DIM
    sb_w = SB_HEADS * SB_HEAD_DIM
    assert sgu_w == sb_w
    off_q = 2 * sgu_w
    off_g = off_q + 3 * sb_w
    outs = []
    for b in range(batch):
        h = x.reshape(seq, d) if batch == 1 else x[b]
        for l in range(depth):
            w_first = w_in[l][:, :off_g].astype(BF16)
            xn, a, qkv = _sgu(h, norm_mix[l], w_first, ln_v_g[l], ln_v_b[l], w_s[l], b_s[l],
                              _pick(seq, SGU_ROWS))
            tn = _pick(d, GATE_COLS)
            whole = lambda wt: (wt, wt.shape[1], 0)
            gate_cols = [(w_in[l], tn, off_g // tn + k) for k in range(2 * d // tn)]
            o_b, bf = _attention(qkv, [whole(w_proj_a[l]), whole(w_proj_b[l]), whole(w_out[l]),
                                       whole(w_gate_up[l]), whole(w_down[l])] + gate_cols)
            wa_l, wb_l, wo_l, wgu_l, wd_l = bf[:5]
            h, hn = _gated_merge(xn, a, o_b, h, bf[5:], b_gate[l], wa_l, wb_l, wo_l,
                                 norm_ffn[l], _pick(seq, MERGE_ROWS))
            h = _ffn(hn, h, wgu_l, wd_l, norm_final, _pick(seq, FFN_UP_ROWS),
                     _pick(w_down.shape[1], FFN_COLS), _pick(seq, FFN_DOWN_ROWS),
                     final_norm=(l == depth - 1))
        outs.append(h)
    return outs[0].reshape(x.shape) if batch == 1 else jnp.stack(outs)
```

```python
import functools
import math

import jax
import jax.numpy as jnp
from jax import lax
from jax.experimental import pallas as pl
from jax.experimental.pallas import tpu as pltpu

EPS = 1e-6
CHUNK = 128
SGU_GROUPS = 8
SGU_GROUP_DIM = 128
SB_HEADS = 8
SB_HEAD_DIM = 128
ATT_BLOCK = 256
SGU_ROWS = 512
MERGE_ROWS = 256
GATE_COLS = 1024
FFN_UP_ROWS = 4096
FFN_UP_GROUP_ROWS = 1024
FFN_COLS = 512
FFN_DOWN_ROWS = 512
FFN_DOWN_GROUP_ROWS = 256
BF16_SUBLANES = 16
STICK_USED_F32_ZERO = 150.0
MASKED_SCORE = -1e30
SOFTPLUS_LINEAR_ABOVE = 90.0
LOG2_E = math.log2(math.e)
VMEM_LIMIT_BYTES = 60 * 1024 * 1024

F32 = jnp.float32
BF16 = jnp.bfloat16


def _params(*sem):
    return pltpu.CompilerParams(dimension_semantics=sem, vmem_limit_bytes=VMEM_LIMIT_BYTES)


def _softplus_bits(z):
    return jnp.where(z > SOFTPLUS_LINEAR_ABOVE, z, jnp.log(1.0 + jnp.exp2(z)) * LOG2_E)


def _gelu(x):
    return 0.5 * x * (1.0 + lax.erf(x * math.sqrt(0.5)))


def _sgu_kernel(x_ref, gn_ref, w_ref, wq_ref, wk_ref, wv_ref, lng_ref, lnb_ref, ws_ref, bs_ref,
                xn_ref, o_ref, qkv_ref):
    width = o_ref.shape[1]
    x = x_ref[...]
    ms = jnp.mean(x * x, axis=-1, keepdims=True)
    xn = (x * lax.rsqrt(ms + EPS) * gn_ref[...]).astype(BF16)
    xn_ref[...] = xn
    v = _gelu(jnp.dot(xn, w_ref[:, width:], preferred_element_type=F32))
    qkv_ref[:, :width] = jnp.dot(xn, wq_ref[...], preferred_element_type=F32).astype(BF16)
    mu = jnp.mean(v, axis=-1, keepdims=True)
    cen = v - mu
    var = jnp.mean(cen * cen, axis=-1, keepdims=True)
    vln = (cen * lax.rsqrt(var + EPS) * lng_ref[...] + lnb_ref[...]).astype(BF16)
    u = _gelu(jnp.dot(xn, w_ref[:, :width], preferred_element_type=F32))
    qkv_ref[:, width:2 * width] = jnp.dot(xn, wk_ref[...], preferred_element_type=F32).astype(BF16)
    qkv_ref[:, 2 * width:] = jnp.dot(xn, wv_ref[...], preferred_element_type=F32).astype(BF16)
    t_idx = lax.broadcasted_iota(jnp.int32, (CHUNK, CHUNK), 0)
    s_idx = lax.broadcasted_iota(jnp.int32, (CHUNK, CHUNK), 1)
    causal = s_idx <= t_idx
    for g in range(SGU_GROUPS):
        wm = jnp.where(causal, ws_ref[g], 0.0).astype(BF16)
        cols = slice(g * SGU_GROUP_DIM, (g + 1) * SGU_GROUP_DIM)
        for c in range(o_ref.shape[0] // CHUNK):
            rows = slice(c * CHUNK, (c + 1) * CHUNK)
            mixed = jnp.dot(wm, vln[rows, cols], preferred_element_type=F32) + bs_ref[g]
            o_ref[rows, cols] = (u[rows, cols] * mixed).astype(o_ref.dtype)


def _sgu(x, g_norm, w_in, ln_g, ln_b, w_s, b_s, tm):
    s, d = x.shape
    width = SGU_GROUPS * SGU_GROUP_DIM
    resident = dict(pipeline_mode=pl.Buffered(1))
    return pl.pallas_call(
        _sgu_kernel,
        grid=(s // tm,),
        in_specs=[pl.BlockSpec((tm, d), lambda i: (i, 0)),
                  pl.BlockSpec((1, d), lambda i: (0, 0)),
                  pl.BlockSpec((d, 2 * width), lambda i: (0, 0), **resident),
                  pl.BlockSpec((d, width), lambda i: (0, 2), **resident),
                  pl.BlockSpec((d, width), lambda i: (0, 3), **resident),
                  pl.BlockSpec((d, width), lambda i: (0, 4), **resident),
                  pl.BlockSpec((1, width), lambda i: (0, 0)),
                  pl.BlockSpec((1, width), lambda i: (0, 0)),
                  pl.BlockSpec(w_s.shape, lambda i: (0, 0, 0)),
                  pl.BlockSpec((SGU_GROUPS, CHUNK, 1), lambda i: (0, 0, 0))],
        out_specs=[pl.BlockSpec((tm, d), lambda i: (i, 0)),
                   pl.BlockSpec((tm, width), lambda i: (i, 0)),
                   pl.BlockSpec((tm, 3 * width), lambda i: (i, 0))],
        out_shape=[jax.ShapeDtypeStruct((s, d), BF16),
                   jax.ShapeDtypeStruct((s, width), BF16),
                   jax.ShapeDtypeStruct((s, 3 * width), BF16)],
        compiler_params=_params("arbitrary"),
        name="sgu_qkv",
    )(x, g_norm.reshape(1, d), w_in, w_in, w_in, w_in, ln_g.reshape(1, width),
      ln_b.reshape(1, width), w_s, b_s.reshape(SGU_GROUPS, CHUNK, 1))


def _attn_kernel(*refs, cast_every):
    n_cast = len(cast_every)
    q_ref, kc_ref, kp_ref, vc_ref, vp_ref, qkv_hbm = refs[:6]
    cast_in = refs[6:6 + n_cast]
    o_ref = refs[6 + n_cast]
    cast_out = refs[7 + n_cast:7 + 2 * n_cast]
    acc_ref, used_ref, kbuf, vbuf, sems = refs[7 + 2 * n_cast:]
    t = kp_ref.shape[0]
    dh = SB_HEAD_DIM
    width = q_ref.shape[1]
    heads = width // dh
    pair = pl.program_id(0)
    scale = LOG2_E / math.sqrt(dh)
    row = lax.broadcasted_iota(jnp.int32, (t, t), 0)
    col = lax.broadcasted_iota(jnp.int32, (t, t), 1)
    later = (row > col).astype(BF16)
    causal = col < row
    no_prev = jnp.where(pair > 0, 0.0, -MASKED_SCORE)
    first, second = slice(0, t), slice(t, 2 * t)

    for src, dst, every in zip(cast_in, cast_out, cast_every):
        if every == 1:
            dst[...] = src[...].astype(BF16)
        else:
            @pl.when(pair % every == 0)
            def _(src=src, dst=dst):
                dst[...] = src[...].astype(BF16)

    def scores(q, k):
        return lax.dot_general(q, k, (((1,), (1,)), ((), ())), preferred_element_type=F32) * scale

    def log_weights(z):
        used = _softplus_bits(z)
        logw = z - used - jnp.dot(used.astype(BF16), later, preferred_element_type=F32)
        return logw, jnp.sum(used, axis=-1, keepdims=True)

    least = [None, None]
    for h in range(heads):
        cols = slice(h * dh, (h + 1) * dh)
        z_a = scores(q_ref[:, cols], kc_ref[first, cols])
        z_ap = scores(q_ref[first, cols], kp_ref[:, cols])
        z_bb = scores(q_ref[second, cols], kc_ref[second, cols])
        logw, used_rows = log_weights(jnp.concatenate(
            [jnp.where(causal, z_a[first], MASKED_SCORE), z_ap,
             jnp.where(causal, z_bb, MASKED_SCORE), z_a[second]], axis=0))
        values = ((vc_ref[first, cols], vp_ref[:, cols]), (vc_ref[second, cols], vc_ref[first, cols]))
        for n, (rows, (v_diag, v_prev)) in enumerate(zip((first, second), values)):
            rows_d = slice(2 * n * t, (2 * n + 1) * t)
            rows_p = slice((2 * n + 1) * t, (2 * n + 2) * t)
            used_d, used_p = used_rows[rows_d], used_rows[rows_p]
            before = used_d + no_prev if n == 0 else used_d
            w = jnp.concatenate([jnp.exp2(logw[rows_d]), jnp.exp2(logw[rows_p] - before)],
                                axis=1).astype(BF16)
            acc_ref[rows, cols] = jnp.dot(w, jnp.concatenate([v_diag, v_prev], axis=0),
                                          preferred_element_type=F32)
            used = used_d + used_p
            used_ref[h, rows] = used
            least[n] = used if least[n] is None else jnp.minimum(least[n], used)

    def walk(rows, first_block, low):
        def cond(state):
            kb, low = state
            return jnp.logical_and(kb >= 0, low <= STICK_USED_F32_ZERO)

        def body(state):
            kb, _ = state
            src_rows = pl.ds(pl.multiple_of(kb * t, t), t)
            k_copy = pltpu.make_async_copy(qkv_hbm.at[src_rows, pl.ds(width, width)], kbuf,
                                           sems.at[0])
            v_copy = pltpu.make_async_copy(qkv_hbm.at[src_rows, pl.ds(2 * width, width)], vbuf,
                                           sems.at[1])
            k_copy.start()
            v_copy.start()
            k_copy.wait()
            v_copy.wait()
            least = None
            for h in range(heads):
                cols = slice(h * dh, (h + 1) * dh)
                logw, used_b = log_weights(scores(q_ref[rows, cols], kbuf[:, cols]))
                used = used_ref[h, rows]
                w = jnp.exp2(logw - used).astype(BF16)
                acc_ref[rows, cols] += jnp.dot(w, vbuf[:, cols], preferred_element_type=F32)
                used = used + used_b
                used_ref[h, rows] = used
                least = used if least is None else jnp.minimum(least, used)
            return kb - 1, jnp.min(least)

        lax.while_loop(cond, body, (first_block, low))

    walk(first, 2 * pair - 2, jnp.min(least[0]))
    walk(second, 2 * pair - 1, jnp.min(least[1]))
    o_ref[...] = acc_ref[...].astype(o_ref.dtype)


def _cast_every(rows, steps):
    every = 1
    while (rows * every) % (steps * BF16_SUBLANES) and every < steps:
        every *= 2
    return every


def _attention(qkv, weights):
    s = qkv.shape[0]
    t = ATT_BLOCK
    w = SB_HEADS * SB_HEAD_DIM
    steps = s // (2 * t)
    before = lambda i: jnp.maximum(2 * i - 1, 0)
    cast_every = tuple(_cast_every(wt.shape[0], steps) for wt, _, _ in weights)
    assert all(steps % e == 0 for e in cast_every)
    cast_in_specs = [pl.BlockSpec((wt.shape[0] * e // steps, nc), lambda i, e=e, cb=cb: (i // e, cb))
                     for (wt, nc, cb), e in zip(weights, cast_every)]
    cast_out_specs = [pl.BlockSpec((wt.shape[0] * e // steps, nc), lambda i, e=e: (i // e, 0))
                      for (wt, nc, _), e in zip(weights, cast_every)]
    outs = pl.pallas_call(
        functools.partial(_attn_kernel, cast_every=cast_every),
        grid=(steps,),
        in_specs=[pl.BlockSpec((2 * t, w), lambda i: (i, 0)),
                  pl.BlockSpec((2 * t, w), lambda i: (i, 1)),
                  pl.BlockSpec((t, w), lambda i: (before(i), 1)),
                  pl.BlockSpec((2 * t, w), lambda i: (i, 2)),
                  pl.BlockSpec((t, w), lambda i: (before(i), 2)),
                  pl.BlockSpec(memory_space=pl.ANY)] + cast_in_specs,
        out_specs=[pl.BlockSpec((2 * t, w), lambda i: (i, 0))] + cast_out_specs,
        out_shape=[jax.ShapeDtypeStruct((s, w), BF16)]
        + [jax.ShapeDtypeStruct((wt.shape[0], nc), BF16) for wt, nc, _ in weights],
        scratch_shapes=[pltpu.VMEM((2 * t, w), F32), pltpu.VMEM((SB_HEADS, 2 * t, 1), F32),
                        pltpu.VMEM((t, w), BF16), pltpu.VMEM((t, w), BF16),
                        pltpu.SemaphoreType.DMA((2,))],
        compiler_params=_params("arbitrary"),
        name="sb_attention",
    )(qkv, qkv, qkv, qkv, qkv, qkv, *[wt for wt, _, _ in weights])
    return outs[0], outs[1:]


def _ffn_up_kernel(hn_ref, wg_ref, wu_ref, h_ref):
    rows_per_group = _pick(hn_ref.shape[0], FFN_UP_GROUP_ROWS)
    for r in range(0, hn_ref.shape[0], rows_per_group):
        rows = slice(r, r + rows_per_group)
        hn = hn_ref[rows, :]
        gate = jnp.dot(hn, wg_ref[...], preferred_element_type=F32)
        up = jnp.dot(hn, wu_ref[...], preferred_element_type=F32)
        h_ref[rows, :] = (gate * jax.nn.sigmoid(gate) * up).astype(BF16)


def _ffn_down_kernel(h_ref, x_ref, wd_ref, gf_ref, o_ref, *, final_norm):
    rows_per_group = _pick(x_ref.shape[0], FFN_DOWN_GROUP_ROWS)
    for r in range(0, x_ref.shape[0], rows_per_group):
        rows = slice(r, r + rows_per_group)
        y = x_ref[rows, :] + jnp.dot(h_ref[rows, :], wd_ref[...], preferred_element_type=F32)
        if final_norm:
            ms = jnp.mean(y * y, axis=-1, keepdims=True)
            y = y * lax.rsqrt(ms + EPS) * gf_ref[...]
        o_ref[rows, :] = y


def _ffn(hn, x, w_gate_up, w_down, g_final, tm_up, tf, tm_down, final_norm):
    s, d = x.shape
    d_ff = w_down.shape[0]
    nf = d_ff // tf
    h = pl.pallas_call(
        _ffn_up_kernel,
        grid=(s // tm_up, nf),
        in_specs=[pl.BlockSpec((tm_up, d), lambda i, f: (i, 0)),
                  pl.BlockSpec((d, tf), lambda i, f: (0, f)),
                  pl.BlockSpec((d, tf), lambda i, f: (0, nf + f))],
        out_specs=pl.BlockSpec((tm_up, tf), lambda i, f: (i, f)),
        out_shape=jax.ShapeDtypeStruct((s, d_ff), BF16),
        compiler_params=_params("arbitrary", "arbitrary"),
        name="ffn_up",
    )(hn, w_gate_up, w_gate_up)
    return pl.pallas_call(
        functools.partial(_ffn_down_kernel, final_norm=final_norm),
        grid=(s // tm_down,),
        in_specs=[pl.BlockSpec((tm_down, d_ff), lambda i: (i, 0)),
                  pl.BlockSpec((tm_down, d), lambda i: (i, 0)),
                  pl.BlockSpec((d_ff, d), lambda i: (0, 0), pipeline_mode=pl.Buffered(1)),
                  pl.BlockSpec((1, d), lambda i: (0, 0))],
        out_specs=pl.BlockSpec((tm_down, d), lambda i: (i, 0)),
        out_shape=jax.ShapeDtypeStruct((s, d), F32),
        compiler_params=_params("arbitrary"),
        name="ffn_down",
    )(h, x, w_down, g_final.reshape(1, d))


def _gated_merge_kernel(xn_ref, a_ref, b_ref, x_ref, *refs):
    o_ref, hn_ref = refs[-2:]
    bg_ref, wa_ref, wb_ref, wo_ref, gn_ref = refs[-7:-2]
    wg_refs = refs[:-7]
    nc = len(wg_refs) // 2
    d = x_ref.shape[1]
    tn = d // nc
    xn = xn_ref[...]
    a = a_ref[...]
    b = b_ref[...]
    parts = []
    for c in range(nc):
        cols = slice(c * tn, (c + 1) * tn)
        g_a = jax.nn.sigmoid(jnp.dot(xn, wg_refs[c][...], preferred_element_type=F32)
                             + bg_ref[:, cols])
        g_b = jax.nn.sigmoid(jnp.dot(xn, wg_refs[nc + c][...], preferred_element_type=F32)
                             + bg_ref[:, d + c * tn:d + (c + 1) * tn])
        y_a = jnp.dot(a, wa_ref[:, cols], preferred_element_type=F32)
        y_b = jnp.dot(b, wb_ref[:, cols], preferred_element_type=F32)
        parts.append((g_a * y_a + g_b * y_b).astype(BF16))
    m = jnp.concatenate(parts, axis=1)
    y = x_ref[...] + jnp.dot(m, wo_ref[...], preferred_element_type=F32)
    o_ref[...] = y
    ms = jnp.mean(y * y, axis=-1, keepdims=True)
    hn_ref[...] = (y * lax.rsqrt(ms + EPS) * gn_ref[...]).astype(BF16)


def _gated_merge(xn, a, b, x, gate_tiles, b_gate, w_a, w_b, w_o, g_norm, tm):
    s, d = x.shape
    wa = a.shape[1]
    wb = b.shape[1]
    resident = dict(pipeline_mode=pl.Buffered(1))
    gate_specs = [pl.BlockSpec(g.shape, lambda i: (0, 0), **resident) for g in gate_tiles]
    return pl.pallas_call(
        _gated_merge_kernel,
        grid=(s // tm,),
        in_specs=[pl.BlockSpec((tm, d), lambda i: (i, 0)),
                  pl.BlockSpec((tm, wa), lambda i: (i, 0)),
                  pl.BlockSpec((tm, wb), lambda i: (i, 0)),
                  pl.BlockSpec((tm, d), lambda i: (i, 0))]
        + gate_specs
        + [pl.BlockSpec((1, 2 * d), lambda i: (0, 0)),
           pl.BlockSpec((wa, d), lambda i: (0, 0), **resident),
           pl.BlockSpec((wb, d), lambda i: (0, 0), **resident),
           pl.BlockSpec((d, d), lambda i: (0, 0), **resident),
           pl.BlockSpec((1, d), lambda i: (0, 0))],
        out_specs=[pl.BlockSpec((tm, d), lambda i: (i, 0)),
                   pl.BlockSpec((tm, d), lambda i: (i, 0))],
        out_shape=[jax.ShapeDtypeStruct((s, d), F32),
                   jax.ShapeDtypeStruct((s, d), BF16)],
        compiler_params=_params("arbitrary"),
        name="gated_merge",
    )(xn, a, b, x, *gate_tiles, b_gate.reshape(1, 2 * d), w_a, w_b, w_o, g_norm.reshape(1, d))


def _pick(n, pref):
    return pref if n % pref == 0 else n


def kernel(x, norm_mix, w_in, b_gate, ln_v_g, ln_v_b, w_s, b_s, w_proj_a, w_proj_b,
           w_out, norm_ffn, w_gate_up, w_down, norm_final):
    batch, seq, d = x.shape
    depth = w_in.shape[0]
    sgu_w = SGU_GROUPS * SGU_GROUP_DIM
    sb_w = SB_HEADS * SB_HEAD_DIM
    assert sgu_w == sb_w
    off_q = 2 * sgu_w
    off_g = off_q + 3 * sb_w
    outs = []
    for b in range(batch):
        h = x.reshape(seq, d) if batch == 1 else x[b]
        for l in range(depth):
            w_first = w_in[l][:, :off_g].astype(BF16)
            xn, a, qkv = _sgu(h, norm_mix[l], w_first, ln_v_g[l], ln_v_b[l], w_s[l], b_s[l],
                              _pick(seq, SGU_ROWS))
            tn = _pick(d, GATE_COLS)
            whole = lambda wt: (wt, wt.shape[1], 0)
            gate_cols = [(w_in[l], tn, off_g // tn + k) for k in range(2 * d // tn)]
            o_b, bf = _attention(qkv, [whole(w_proj_a[l]), whole(w_proj_b[l]), whole(w_out[l]),
                                       whole(w_gate_up[l]), whole(w_down[l])] + gate_cols)
            wa_l, wb_l, wo_l, wgu_l, wd_l = bf[:5]
            h, hn = _gated_merge(xn, a, o_b, h, bf[5:], b_gate[l], wa_l, wb_l, wo_l,
                                 norm_ffn[l], _pick(seq, MERGE_ROWS))
            h = _ffn(hn, h, wgu_l, wd_l, norm_final, _pick(seq, FFN_UP_ROWS),
                     _pick(w_down.shape[1], FFN_COLS), _pick(seq, FFN_DOWN_ROWS),
                     final_norm=(l == depth - 1))
        outs.append(h)
    return outs[0].reshape(x.shape) if batch == 1 else jnp.stack(outs)
```

```python
import functools
import math

import jax
import jax.numpy as jnp
from jax import lax
from jax.experimental import pallas as pl
from jax.experimental.pallas import tpu as pltpu

EPS = 1e-6
CHUNK = 128
SGU_GROUPS = 8
SGU_GROUP_DIM = 128
SB_HEADS = 8
SB_HEAD_DIM = 128
ATT_BLOCK = 256
SGU_ROWS = 512
MERGE_ROWS = 256
GATE_COLS = 1024
FFN_UP_ROWS = 4096
FFN_UP_GROUP_ROWS = 1024
FFN_COLS = 512
FFN_DOWN_ROWS = 512
FFN_DOWN_GROUP_ROWS = 256
BF16_SUBLANES = 16
STICK_USED_F32_ZERO = 150.0
MASKED_SCORE = -1e30
SOFTPLUS_LINEAR_ABOVE = 90.0
LOG2_E = math.log2(math.e)
VMEM_LIMIT_BYTES = 60 * 1024 * 1024

F32 = jnp.float32
BF16 = jnp.bfloat16


def _params(*sem):
    return pltpu.CompilerParams(dimension_semantics=sem, vmem_limit_bytes=VMEM_LIMIT_BYTES)


def _softplus_bits(z):
    return jnp.where(z > SOFTPLUS_LINEAR_ABOVE, z, jnp.log(1.0 + jnp.exp2(z)) * LOG2_E)


def _gelu(x):
    return 0.5 * x * (1.0 + lax.erf(x * math.sqrt(0.5)))


def _sgu_kernel(x_ref, gn_ref, w_ref, wq_ref, wk_ref, wv_ref, lng_ref, lnb_ref, ws_ref, bs_ref,
                xn_ref, o_ref, qkv_ref):
    width = o_ref.shape[1]
    x = x_ref[...]
    ms = jnp.mean(x * x, axis=-1, keepdims=True)
    xn = (x * lax.rsqrt(ms + EPS) * gn_ref[...]).astype(BF16)
    xn_ref[...] = xn
    v = _gelu(jnp.dot(xn, w_ref[:, width:], preferred_element_type=F32))
    qkv_ref[:, :width] = jnp.dot(xn, wq_ref[...], preferred_element_type=F32).astype(BF16)
    mu = jnp.mean(v, axis=-1, keepdims=True)
    cen = v - mu
    var = jnp.mean(cen * cen, axis=-1, keepdims=True)
    vln = (cen * lax.rsqrt(var + EPS) * lng_ref[...] + lnb_ref[...]).astype(BF16)
    u = _gelu(jnp.dot(xn, w_ref[:, :width], preferred_element_type=F32))
    qkv_ref[:, width:2 * width] = jnp.dot(xn, wk_ref[...], preferred_element_type=F32).astype(BF16)
    qkv_ref[:, 2 * width:] = jnp.dot(xn, wv_ref[...], preferred_element_type=F32).astype(BF16)
    t_idx = lax.broadcasted_iota(jnp.int32, (CHUNK, CHUNK), 0)
    s_idx = lax.broadcasted_iota(jnp.int32, (CHUNK, CHUNK), 1)
    causal = s_idx <= t_idx
    for g in range(SGU_GROUPS):
        wm = jnp.where(causal, ws_ref[g], 0.0).astype(BF16)
        cols = slice(g * SGU_GROUP_DIM, (g + 1) * SGU_GROUP_DIM)
        for c in range(o_ref.shape[0] // CHUNK):
            rows = slice(c * CHUNK, (c + 1) * CHUNK)
            mixed = jnp.dot(wm, vln[rows, cols], preferred_element_type=F32) + bs_ref[g]
            o_ref[rows, cols] = (u[rows, cols] * mixed).astype(o_ref.dtype)


def _sgu(x, g_norm, w_in, ln_g, ln_b, w_s, b_s, tm):
    s, d = x.shape
    width = SGU_GROUPS * SGU_GROUP_DIM
    resident = dict(pipeline_mode=pl.Buffered(1))
    return pl.pallas_call(
        _sgu_kernel,
        grid=(s // tm,),
        in_specs=[pl.BlockSpec((tm, d), lambda i: (i, 0)),
                  pl.BlockSpec((1, d), lambda i: (0, 0)),
                  pl.BlockSpec((d, 2 * width), lambda i: (0, 0), **resident),
                  pl.BlockSpec((d, width), lambda i: (0, 2), **resident),
                  pl.BlockSpec((d, width), lambda i: (0, 3), **resident),
                  pl.BlockSpec((d, width), lambda i: (0, 4), **resident),
                  pl.BlockSpec((1, width), lambda i: (0, 0)),
                  pl.BlockSpec((1, width), lambda i: (0, 0)),
                  pl.BlockSpec(w_s.shape, lambda i: (0, 0, 0)),
                  pl.BlockSpec((SGU_GROUPS, CHUNK, 1), lambda i: (0, 0, 0))],
        out_specs=[pl.BlockSpec((tm, d), lambda i: (i, 0)),
                   pl.BlockSpec((tm, width), lambda i: (i, 0)),
                   pl.BlockSpec((tm, 3 * width), lambda i: (i, 0))],
        out_shape=[jax.ShapeDtypeStruct((s, d), BF16),
                   jax.ShapeDtypeStruct((s, width), BF16),
                   jax.ShapeDtypeStruct((s, 3 * width), BF16)],
        compiler_params=_params("arbitrary"),
        name="sgu_qkv",
    )(x, g_norm.reshape(1, d), w_in, w_in, w_in, w_in, ln_g.reshape(1, width),
      ln_b.reshape(1, width), w_s, b_s.reshape(SGU_GROUPS, CHUNK, 1))


def _attn_kernel(*refs, cast_every):
    n_cast = len(cast_every)
    q_ref, kc_ref, kp_ref, vc_ref, vp_ref, qkv_hbm = refs[:6]
    cast_in = refs[6:6 + n_cast]
    o_ref = refs[6 + n_cast]
    cast_out = refs[7 + n_cast:7 + 2 * n_cast]
    acc_ref, used_ref, kbuf, vbuf, sems = refs[7 + 2 * n_cast:]
    t = kp_ref.shape[0]
    dh = SB_HEAD_DIM
    width = q_ref.shape[1]
    heads = width // dh
    pair = pl.program_id(0)
    scale = LOG2_E / math.sqrt(dh)
    row = lax.broadcasted_iota(jnp.int32, (t, t), 0)
    col = lax.broadcasted_iota(jnp.int32, (t, t), 1)
    later = (row > col).astype(BF16)
    causal = col < row
    no_prev = jnp.where(pair > 0, 0.0, -MASKED_SCORE)
    first, second = slice(0, t), slice(t, 2 * t)

    for src, dst, every in zip(cast_in, cast_out, cast_every):
        if every == 1:
            dst[...] = src[...].astype(BF16)
        else:
            @pl.when(pair % every == 0)
            def _(src=src, dst=dst):
                dst[...] = src[...].astype(BF16)

    def scores(q, k):
        return lax.dot_general(q, k, (((1,), (1,)), ((), ())), preferred_element_type=F32) * scale

    def log_weights(z):
        used = _softplus_bits(z)
        logw = z - used - jnp.dot(used.astype(BF16), later, preferred_element_type=F32)
        return logw, jnp.sum(used, axis=-1, keepdims=True)

    least = [None, None]
    for h in range(heads):
        cols = slice(h * dh, (h + 1) * dh)
        z_a = scores(q_ref[:, cols], kc_ref[first, cols])
        z_ap = scores(q_ref[first, cols], kp_ref[:, cols])
        z_bb = scores(q_ref[second, cols], kc_ref[second, cols])
        logw, used_rows = log_weights(jnp.concatenate(
            [jnp.where(causal, z_a[first], MASKED_SCORE), z_ap,
             jnp.where(causal, z_bb, MASKED_SCORE), z_a[second]], axis=0))
        values = ((vc_ref[first, cols], vp_ref[:, cols]), (vc_ref[second, cols], vc_ref[first, cols]))
        for n, (rows, (v_diag, v_prev)) in enumerate(zip((first, second), values)):
            rows_d = slice(2 * n * t, (2 * n + 1) * t)
            rows_p = slice((2 * n + 1) * t, (2 * n + 2) * t)
            used_d, used_p = used_rows[rows_d], used_rows[rows_p]
            before = used_d + no_prev if n == 0 else used_d
            w = jnp.concatenate([jnp.exp2(logw[rows_d]), jnp.exp2(logw[rows_p] - before)],
                                axis=1).astype(BF16)
            acc_ref[rows, cols] = jnp.dot(w, jnp.concatenate([v_diag, v_prev], axis=0),
                                          preferred_element_type=F32)
            used = used_d + used_p
            used_ref[h, rows] = used
            least[n] = used if least[n] is None else jnp.minimum(least[n], used)

    def walk(rows, first_block, low):
        def cond(state):
            kb, low = state
            return jnp.logical_and(kb >= 0, low <= STICK_USED_F32_ZERO)

        def body(state):
            kb, _ = state
            src_rows = pl.ds(pl.multiple_of(kb * t, t), t)
            k_copy = pltpu.make_async_copy(qkv_hbm.at[src_rows, pl.ds(width, width)], kbuf,
                                           sems.at[0])
            v_copy = pltpu.make_async_copy(qkv_hbm.at[src_rows, pl.ds(2 * width, width)], vbuf,
                                           sems.at[1])
            k_copy.start()
            v_copy.start()
            k_copy.wait()
            v_copy.wait()
            least = None
            for h in range(heads):
                cols = slice(h * dh, (h + 1) * dh)
                logw, used_b = log_weights(scores(q_ref[rows, cols], kbuf[:, cols]))
                used = used_ref[h, rows]
                w = jnp.exp2(logw - used).astype(BF16)
                acc_ref[rows, cols] += jnp.dot(w, vbuf[:, cols], preferred_element_type=F32)
                used = used + used_b
                used_ref[h, rows] = used
                least = used if least is None else jnp.minimum(least, used)
            return kb - 1, jnp.min(least)

        lax.while_loop(cond, body, (first_block, low))

    walk(first, 2 * pair - 2, jnp.min(least[0]))
    walk(second, 2 * pair - 1, jnp.min(least[1]))
    o_ref[...] = acc_ref[...].astype(o_ref.dtype)


def _cast_every(rows, steps):
    every = 1
    while (rows * every) % (steps * BF16_SUBLANES) and every < steps:
        every *= 2
    return every


def _attention(qkv, weights):
    s = qkv.shape[0]
    t = ATT_BLOCK
    w = SB_HEADS * SB_HEAD_DIM
    steps = s // (2 * t)
    before = lambda i: jnp.maximum(2 * i - 1, 0)
    cast_every = tuple(_cast_every(wt.shape[0], steps) for wt, _, _ in weights)
    assert all(steps % e == 0 for e in cast_every)
    cast_in_specs = [pl.BlockSpec((wt.shape[0] * e // steps, nc), lambda i, e=e, cb=cb: (i // e, cb))
                     for (wt, nc, cb), e in zip(weights, cast_every)]
    cast_out_specs = [pl.BlockSpec((wt.shape[0] * e // steps, nc), lambda i, e=e: (i // e, 0))
                      for (wt, nc, _), e in zip(weights, cast_every)]
    outs = pl.pallas_call(
        functools.partial(_attn_kernel, cast_every=cast_every),
        grid=(steps,),
        in_specs=[pl.BlockSpec((2 * t, w), lambda i: (i, 0)),
                  pl.BlockSpec((2 * t, w), lambda i: (i, 1)),
                  pl.BlockSpec((t, w), lambda i: (before(i), 1)),
                  pl.BlockSpec((2 * t, w), lambda i: (i, 2)),
                  pl.BlockSpec((t, w), lambda i: (before(i), 2)),
                  pl.BlockSpec(memory_space=pl.ANY)] + cast_in_specs,
        out_specs=[pl.BlockSpec((2 * t, w), lambda i: (i, 0))] + cast_out_specs,
        out_shape=[jax.ShapeDtypeStruct((s, w), BF16)]
        + [jax.ShapeDtypeStruct((wt.shape[0], nc), BF16) for wt, nc, _ in weights],
        scratch_shapes=[pltpu.VMEM((2 * t, w), F32), pltpu.VMEM((SB_HEADS, 2 * t, 1), F32),
                        pltpu.VMEM((t, w), BF16), pltpu.VMEM((t, w), BF16),
                        pltpu.SemaphoreType.DMA((2,))],
        compiler_params=_params("arbitrary"),
        name="sb_attention",
    )(qkv, qkv, qkv, qkv, qkv, qkv, *[wt for wt, _, _ in weights])
    return outs[0], outs[1:]


def _ffn_up_kernel(hn_ref, wg_ref, wu_ref, h_ref):
    rows_per_group = _pick(hn_ref.shape[0], FFN_UP_GROUP_ROWS)
    for r in range(0, hn_ref.shape[0], rows_per_group):
        rows = slice(r, r + rows_per_group)
        hn = hn_ref[rows, :]
        gate = jnp.dot(hn, wg_ref[...], preferred_element_type=F32)
        up = jnp.dot(hn, wu_ref[...], preferred_element_type=F32)
        h_ref[rows, :] = (gate * jax.nn.sigmoid(gate) * up).astype(BF16)


def _ffn_down_kernel(h_ref, x_ref, wd_ref, gf_ref, o_ref, *, final_norm):
    rows_per_group = _pick(x_ref.shape[0], FFN_DOWN_GROUP_ROWS)
    for r in range(0, x_ref.shape[0], rows_per_group):
        rows = slice(r, r + rows_per_group)
        y = x_ref[rows, :] + jnp.dot(h_ref[rows, :], wd_ref[...], preferred_element_type=F32)
        if final_norm:
            ms = jnp.mean(y * y, axis=-1, keepdims=True)
            y = y * lax.rsqrt(ms + EPS) * gf_ref[...]
        o_ref[rows, :] = y


def _ffn(hn, x, w_gate_up, w_down, g_final, tm_up, tf, tm_down, final_norm):
    s, d = x.shape
    d_ff = w_down.shape[0]
    nf = d_ff // tf
    h = pl.pallas_call(
        _ffn_up_kernel,
        grid=(nf, s // tm_up),
        in_specs=[pl.BlockSpec((tm_up, d), lambda f, i: (i, 0)),
                  pl.BlockSpec((d, tf), lambda f, i: (0, f)),
                  pl.BlockSpec((d, tf), lambda f, i: (0, nf + f))],
        out_specs=pl.BlockSpec((tm_up, tf), lambda f, i: (i, f)),
        out_shape=jax.ShapeDtypeStruct((s, d_ff), BF16),
        compiler_params=_params("arbitrary", "arbitrary"),
        name="ffn_up",
    )(hn, w_gate_up, w_gate_up)
    return pl.pallas_call(
        functools.partial(_ffn_down_kernel, final_norm=final_norm),
        grid=(s // tm_down,),
        in_specs=[pl.BlockSpec((tm_down, d_ff), lambda i: (i, 0)),
                  pl.BlockSpec((tm_down, d), lambda i: (i, 0)),
                  pl.BlockSpec((d_ff, d), lambda i: (0, 0), pipeline_mode=pl.Buffered(1)),
                  pl.BlockSpec((1, d), lambda i: (0, 0))],
        out_specs=pl.BlockSpec((tm_down, d), lambda i: (i, 0)),
        out_shape=jax.ShapeDtypeStruct((s, d), F32),
        compiler_params=_params("arbitrary"),
        name="ffn_down",
    )(h, x, w_down, g_final.reshape(1, d))


def _gated_merge_kernel(xn_ref, a_ref, b_ref, x_ref, *refs):
    o_ref, hn_ref = refs[-2:]
    bg_ref, wa_ref, wb_ref, wo_ref, gn_ref = refs[-7:-2]
    wg_refs = refs[:-7]
    nc = len(wg_refs) // 2
    d = x_ref.shape[1]
    tn = d // nc
    xn = xn_ref[...]
    a = a_ref[...]
    b = b_ref[...]
    parts = []
    for c in range(nc):
        cols = slice(c * tn, (c + 1) * tn)
        g_a = jax.nn.sigmoid(jnp.dot(xn, wg_refs[c][...], preferred_element_type=F32)
                             + bg_ref[:, cols])
        g_b = jax.nn.sigmoid(jnp.dot(xn, wg_refs[nc + c][...], preferred_element_type=F32)
                             + bg_ref[:, d + c * tn:d + (c + 1) * tn])
        y_a = jnp.dot(a, wa_ref[:, cols], preferred_element_type=F32)
        y_b = jnp.dot(b, wb_ref[:, cols], preferred_element_type=F32)
        parts.append((g_a * y_a + g_b * y_b).astype(BF16))
    m = jnp.concatenate(parts, axis=1)
    y = x_ref[...] + jnp.dot(m, wo_ref[...], preferred_element_type=F32)
    o_ref[...] = y
    ms = jnp.mean(y * y, axis=-1, keepdims=True)
    hn_ref[...] = (y * lax.rsqrt(ms + EPS) * gn_ref[...]).astype(BF16)


def _gated_merge(xn, a, b, x, gate_tiles, b_gate, w_a, w_b, w_o, g_norm, tm):
    s, d = x.shape
    wa = a.shape[1]
    wb = b.shape[1]
    resident = dict(pipeline_mode=pl.Buffered(1))
    gate_specs = [pl.BlockSpec(g.shape, lambda i: (0, 0), **resident) for g in gate_tiles]
    return pl.pallas_call(
        _gated_merge_kernel,
        grid=(s // tm,),
        in_specs=[pl.BlockSpec((tm, d), lambda i: (i, 0)),
                  pl.BlockSpec((tm, wa), lambda i: (i, 0)),
                  pl.BlockSpec((tm, wb), lambda i: (i, 0)),
                  pl.BlockSpec((tm, d), lambda i: (i, 0))]
        + gate_specs
        + [pl.BlockSpec((1, 2 * d), lambda i: (0, 0)),
           pl.BlockSpec((wa, d), lambda i: (0, 0), **resident),
           pl.BlockSpec((wb, d), lambda i: (0, 0), **resident),
           pl.BlockSpec((d, d), lambda i: (0, 0), **resident),
           pl.BlockSpec((1, d), lambda i: (0, 0))],
        out_specs=[pl.BlockSpec((tm, d), lambda i: (i, 0)),
                   pl.BlockSpec((tm, d), lambda i: (i, 0))],
        out_shape=[jax.ShapeDtypeStruct((s, d), F32),
                   jax.ShapeDtypeStruct((s, d), BF16)],
        compiler_params=_params("arbitrary"),
        name="gated_merge",
    )(xn, a, b, x, *gate_tiles, b_gate.reshape(1, 2 * d), w_a, w_b, w_o, g_norm.reshape(1, d))


def _pick(n, pref):
    return pref if n % pref == 0 else n


def kernel(x, norm_mix, w_in, b_gate, ln_v_g, ln_v_b, w_s, b_s, w_proj_a, w_proj_b,
           w_out, norm_ffn, w_gate_up, w_down, norm_final):
    batch, seq, d = x.shape
    depth = w_in.shape[0]
    sgu_w = SGU_GROUPS * SGU_GROUP_DIM
    sb_w = SB_HEADS * SB_HEAD_DIM
    assert sgu_w == sb_w
    off_q = 2 * sgu_w
    off_g = off_q + 3 * sb_w
    outs = []
    for b in range(batch):
        h = x.reshape(seq, d) if batch == 1 else x[b]
        for l in range(depth):
            w_first = w_in[l][:, :off_g].astype(BF16)
            xn, a, qkv = _sgu(h, norm_mix[l], w_first, ln_v_g[l], ln_v_b[l], w_s[l], b_s[l],
                              _pick(seq, SGU_ROWS))
            tn = _pick(d, GATE_COLS)
            whole = lambda wt: (wt, wt.shape[1], 0)
            gate_cols = [(w_in[l], tn, off_g // tn + k) for k in range(2 * d // tn)]
            o_b, bf = _attention(qkv, [whole(w_proj_a[l]), whole(w_proj_b[l]), whole(w_out[l]),
                                       whole(w_gate_up[l]), whole(w_down[l])] + gate_cols)
            wa_l, wb_l, wo_l, wgu_l, wd_l = bf[:5]
            h, hn = _gated_merge(xn, a, o_b, h, bf[5:], b_gate[l], wa_l, wb_l, wo_l,
                                 norm_ffn[l], _pick(seq, MERGE_ROWS))
            h = _ffn(hn, h, wgu_l, wd_l, norm_final, _pick(seq, FFN_UP_ROWS),
                     _pick(w_down.shape[1], FFN_COLS), _pick(seq, FFN_DOWN_ROWS),
                     final_norm=(l == depth - 1))
        outs.append(h)
    return outs[0].reshape(x.shape) if batch == 1 else jnp.stack(outs)
```
